```python
import math
import jax, jax.numpy as jnp
from jax import lax
import numpy as np

D_MODEL = 2048
BATCH = 4
SEQ = 4096
DEPTH = 1

D_MIX = D_MODEL
D_POOL = D_MIX // 2
POOL_WINDOWS = (2, 4, 8, 16)
N_POOL_GROUPS = len(POOL_WINDOWS)
POOL_GROUP = D_POOL // N_POOL_GROUPS
D_ATTN = D_MIX - D_POOL
N_HEADS = 8
HEAD_DIM = D_ATTN // N_HEADS
IDX_HEADS = 16
IDX_DIM = 64
INDEXER_SCALE = (IDX_HEADS * IDX_DIM) ** -0.5
MAX_TOPK = 256
Q_BLOCK = 128
N_BUCKETS = 32
MAX_DISTANCE = 128
PEER_HEADS = 8
PEER_KEYS = 128
N_EXPERTS = PEER_KEYS * PEER_KEYS
PEER_QDIM = 256
PEER_HALF = PEER_QDIM // 2
PEER_TOPK = 16
PEER_CHUNK = 128
ALPHA = (2 * DEPTH) ** 0.25
BETA = (8 * DEPTH) ** -0.25
LN_EPS = 1e-5
NEG_INF = -1e30

SPLIT_SIZES = (D_POOL, D_ATTN, D_ATTN, D_ATTN, IDX_HEADS * IDX_DIM, IDX_DIM, IDX_HEADS)
D_IN = sum(SPLIT_SIZES)

kernel_name = "hymba_pool_dsa_peer_deepnorm"


def layer_norm(x, g, b):
    xf = x.astype(jnp.float32)
    mu = jnp.mean(xf, axis=-1, keepdims=True)
    var = jnp.mean(jnp.square(xf - mu), axis=-1, keepdims=True)
    y = (xf - mu) * lax.rsqrt(var + LN_EPS)
    return (y * g.astype(jnp.float32) + b.astype(jnp.float32)).astype(x.dtype)


def split_columns(proj):
    parts, start = [], 0
    for size in SPLIT_SIZES:
        parts.append(proj[..., start:start + size])
        start += size
    return parts


def pool_mixer(v, pool_w, pool_scale):
    B, S, _ = v.shape
    vg = v.reshape(B, S, N_POOL_GROUPS, POOL_GROUP)
    c = jnp.cumsum(vg.astype(jnp.float32), axis=1)
    c = jnp.pad(c, ((0, 0), (1, 0), (0, 0), (0, 0)))
    pos = jnp.arange(S)
    means = []
    for g, w in enumerate(POOL_WINDOWS):
        cg = c[:, :, g]
        hi = cg[:, 1:]
        lo = cg[:, jnp.maximum(pos + 1 - w, 0)]
        cnt = jnp.minimum(pos + 1, w).astype(jnp.float32)[None, :, None]
        means.append((hi - lo) / cnt)
    pooled = jnp.stack(means, axis=2).astype(v.dtype) - vg
    mixed = jnp.einsum('bsgc,gcd->bsgd', pooled, pool_w)
    return mixed.reshape(B, S, D_POOL) * pool_scale


def t5_bucket(dist):
    max_exact = N_BUCKETS // 2
    d = jnp.maximum(dist, 1).astype(jnp.float32)
    large = max_exact + (jnp.log(d / max_exact) / math.log(MAX_DISTANCE / max_exact)
                         * (N_BUCKETS - max_exact)).astype(jnp.int32)
    large = jnp.minimum(large, N_BUCKETS - 1)
    return jnp.where(dist < max_exact, dist, large)


def sparse_attention(q, k, v, q_idx, k_idx, w_idx, rel_bias):
    B, S = q.shape[:2]
    top_k = min(MAX_TOPK, S // 4)
    n_blocks = S // Q_BLOCK
    b_ix = jnp.arange(B)[:, None, None]
    key_pos = jnp.arange(S)

    def block(i):
        start = i * Q_BLOCK
        qi = lax.dynamic_slice_in_dim(q_idx, start, Q_BLOCK, axis=1)
        wi = lax.dynamic_slice_in_dim(w_idx, start, Q_BLOCK, axis=1)
        qa = lax.dynamic_slice_in_dim(q, start, Q_BLOCK, axis=1)
        q_pos = start + jnp.arange(Q_BLOCK)
        rel = jax.nn.relu(jnp.einsum('bqhd,bsd->bqhs', qi, k_idx))
        score = jnp.einsum('bqh,bqhs->bqs', wi, rel).astype(jnp.float32) * INDEXER_SCALE
        causal = key_pos[None, :] <= q_pos[:, None]
        score = jnp.where(causal[None], score, NEG_INF)
        _, sel = lax.top_k(score, top_k)
        k_sel = k[b_ix, sel]
        v_sel = v[b_ix, sel]
        logits = jnp.einsum('bqhd,bqkhd->bhqk', qa, k_sel).astype(jnp.float32) * HEAD_DIM ** -0.5
        dist = q_pos[None, :, None] - sel
        bias = rel_bias[t5_bucket(jnp.maximum(dist, 0))]
        logits = logits + jnp.transpose(bias, (0, 3, 1, 2)).astype(jnp.float32)
        logits = jnp.where((dist >= 0)[:, None], logits, NEG_INF)
        p = jax.nn.softmax(logits, axis=-1).astype(v.dtype)
        return jnp.einsum('bhqk,bqkhd->bqhd', p, v_sel)

    out = lax.map(block, jnp.arange(n_blocks))
    return jnp.transpose(out, (1, 0, 2, 3, 4)).reshape(B, S, N_HEADS * HEAD_DIM)


def peer_ffn(x, wq, sub_keys, u, v):
    B, S, D = x.shape
    xt = x.reshape(-1, PEER_CHUNK, D)

    def chunk(xc):
        qh = (xc @ wq).reshape(PEER_CHUNK, PEER_HEADS, 2, PEER_HALF)
        s1 = jnp.einsum('thd,kd->thk', qh[:, :, 0], sub_keys[0]).astype(jnp.float32)
        s2 = jnp.einsum('thd,kd->thk', qh[:, :, 1], sub_keys[1]).astype(jnp.float32)
        v1, i1 = lax.top_k(s1, PEER_TOPK)
        v2, i2 = lax.top_k(s2, PEER_TOPK)
        cand = (v1[..., :, None] + v2[..., None, :]).reshape(PEER_CHUNK, PEER_HEADS, PEER_TOPK * PEER_TOPK)
        cidx = (i1[..., :, None] * PEER_KEYS + i2[..., None, :]).reshape(PEER_CHUNK, PEER_HEADS, PEER_TOPK * PEER_TOPK)
        best, pos = lax.top_k(cand, PEER_TOPK)
        expert = jnp.take_along_axis(cidx, pos, axis=-1)
        gate = jax.nn.softmax(best, axis=-1)
        u_sel = u[expert]
        v_sel = v[expert]
        act = jax.nn.gelu(jnp.einsum('td,thkd->thk', xc, u_sel).astype(jnp.float32), approximate=False)
        return jnp.einsum('thk,thkd->td', (gate * act).astype(x.dtype), v_sel)

    return lax.map(chunk, xt).reshape(B, S, D)


def setup_inputs(seed: int = 0) -> dict:
    key = jax.random.key(seed)
    ks = jax.random.split(key, 16)
    f32 = jnp.float32
    L = DEPTH
    x = jax.random.normal(ks[0], (BATCH, SEQ, D_MODEL), f32)
    w_in = jax.random.normal(ks[1], (L, D_MODEL, D_IN), f32) * D_MODEL ** -0.5
    pool_w = jax.random.normal(ks[2], (L, N_POOL_GROUPS, POOL_GROUP, POOL_GROUP), f32) * POOL_GROUP ** -0.5
    pool_scale = 1.0 + 0.1 * jax.random.normal(ks[3], (L, D_POOL), f32)
    rel_bias = 0.5 * jax.random.normal(ks[4], (N_BUCKETS, N_HEADS), f32)
    w_out = jax.random.normal(ks[5], (L, D_MIX, D_MODEL), f32) * (D_MIX ** -0.5 * BETA)
    ln1_g = 1.0 + 0.05 * jax.random.normal(ks[6], (L, D_MODEL), f32)
    ln1_b = 0.02 * jax.random.normal(ks[7], (L, D_MODEL), f32)
    peer_wq = jax.random.normal(ks[8], (L, D_MODEL, PEER_HEADS * PEER_QDIM), f32) * D_MODEL ** -0.5
    peer_subkeys = jax.random.normal(ks[9], (L, 2, PEER_KEYS, PEER_HALF), f32) * PEER_HALF ** -0.5
    peer_u = jax.random.normal(ks[10], (L, N_EXPERTS, D_MODEL), f32) * D_MODEL ** -0.5
    peer_v = jax.random.normal(ks[11], (L, N_EXPERTS, D_MODEL), f32) * (BETA * PEER_HEADS ** -0.5)
    ln2_g = 1.0 + 0.05 * jax.random.normal(ks[12], (L, D_MODEL), f32)
    ln2_b = 0.02 * jax.random.normal(ks[13], (L, D_MODEL), f32)
    return {"x": x, "w_in": w_in, "pool_w": pool_w, "pool_scale": pool_scale,
            "rel_bias": rel_bias, "w_out": w_out, "ln1_g": ln1_g, "ln1_b": ln1_b,
            "peer_wq": peer_wq, "peer_subkeys": peer_subkeys, "peer_u": peer_u,
            "peer_v": peer_v, "ln2_g": ln2_g, "ln2_b": ln2_b}


def reference(x, w_in, pool_w, pool_scale, rel_bias, w_out, ln1_g, ln1_b,
              peer_wq, peer_subkeys, peer_u, peer_v, ln2_g, ln2_b):
    B, S, _ = x.shape
    h = x
    for l in range(DEPTH):
        proj = h @ w_in[l]
        p_pool, p_q, p_k, p_v, p_qi, p_ki, p_wi = split_columns(proj)
        out_pool = pool_mixer(p_pool, pool_w[l], pool_scale[l])
        out_attn = sparse_attention(
            p_q.reshape(B, S, N_HEADS, HEAD_DIM),
            p_k.reshape(B, S, N_HEADS, HEAD_DIM),
            p_v.reshape(B, S, N_HEADS, HEAD_DIM),
            p_qi.reshape(B, S, IDX_HEADS, IDX_DIM), p_ki, p_wi, rel_bias)
        mix = jnp.concatenate([out_pool, out_attn], axis=-1) @ w_out[l]
        h = layer_norm(ALPHA * h + mix, ln1_g[l], ln1_b[l])
        ffn = peer_ffn(h, peer_wq[l], peer_subkeys[l], peer_u[l], peer_v[l])
        h = layer_norm(ALPHA * h + ffn, ln2_g[l], ln2_b[l])
    return h
```

```python
import functools
import math

import jax
import jax.numpy as jnp
from jax import lax
from jax.experimental import pallas as pl
from jax.experimental.pallas import tpu as pltpu

F32 = jnp.float32
BF16 = jnp.bfloat16
I32 = jnp.int32

LANES = 128
SUBLANES = 8
VMEM_LIMIT = 56 * 1024 * 1024

POOL_WINDOWS = (2, 4, 8, 16)
N_HEADS = 8
IDX_HEADS = 16
IDX_DIM = 64
MAX_TOPK = 256
N_BUCKETS = 32
MAX_DISTANCE = 128
PEER_HEADS = 8
PEER_KEYS = 128
PEER_TOPK = 16
LN_EPS = 1e-5
NEG_INF = -1e30
M_FLOOR = -1e20
INT_MIN = -(2 ** 31)


def _params(n_grid):
    return pltpu.CompilerParams(dimension_semantics=("arbitrary",) * n_grid,
                                vmem_limit_bytes=VMEM_LIMIT)


def _resident(shape, index_map):
    return pl.BlockSpec(shape, index_map, pipeline_mode=pl.Buffered(1))


def _mm_kernel(a_ref, b_ref, o_ref):
    o_ref[...] = jnp.dot(a_ref[...], b_ref[...], preferred_element_type=F32).astype(o_ref.dtype)


def _matmul(a, b, out_dtype, tm, tn):
    m, k = a.shape
    n = b.shape[1]
    return pl.pallas_call(
        _mm_kernel,
        grid=(n // tn, m // tm),
        in_specs=[pl.BlockSpec((tm, k), lambda j, i: (i, 0)),
                  pl.BlockSpec((k, tn), lambda j, i: (0, j))],
        out_specs=pl.BlockSpec((tm, tn), lambda j, i: (i, j)),
        out_shape=jax.ShapeDtypeStruct((m, n), out_dtype),
        compiler_params=_params(2),
    )(a, b)


def _mm_nt_kernel(w_ref, x_ref, o_ref):
    o_ref[...] = lax.dot_general(w_ref[...], x_ref[...], (((1,), (1,)), ((), ())),
                                 preferred_element_type=F32).astype(o_ref.dtype)


def _matmul_nt(w_t, x, out_dtype, tr, tm):
    r, k = w_t.shape
    m = x.shape[0]
    return pl.pallas_call(
        _mm_nt_kernel,
        grid=(r // tr, m // tm),
        in_specs=[pl.BlockSpec((tr, k), lambda j, i: (j, 0)),
                  pl.BlockSpec((tm, k), lambda j, i: (i, 0))],
        out_specs=pl.BlockSpec((tr, tm), lambda j, i: (j, i)),
        out_shape=jax.ShapeDtypeStruct((r, m), out_dtype),
        compiler_params=_params(2),
    )(w_t, x)


def _mm_nt_blocked_kernel(w_ref, x_ref, o_ref):
    res = lax.dot_general(w_ref[...], x_ref[...], (((1,), (1,)), ((), ())),
                          preferred_element_type=F32).astype(o_ref.dtype)
    for j in range(o_ref.shape[0]):
        o_ref[j] = res[:, j * LANES:(j + 1) * LANES]


def _matmul_nt_blocked(w_t, x, out_dtype, tm):
    r, k = w_t.shape
    m = x.shape[0]
    return pl.pallas_call(
        _mm_nt_blocked_kernel,
        grid=(m // tm,),
        in_specs=[pl.BlockSpec((r, k), lambda i: (0, 0)),
                  pl.BlockSpec((tm, k), lambda i: (i, 0))],
        out_specs=pl.BlockSpec((tm // LANES, r, LANES), lambda i: (i, 0, 0)),
        out_shape=jax.ShapeDtypeStruct((m // LANES, r, LANES), out_dtype),
        compiler_params=_params(1),
    )(w_t, x)


def _pool_kernel(v_ref, halo_ref, w_ref, sc_ref, o_ref, buf_ref, *, seq_tiles, halo):
    ts = v_ref.shape[0]
    group = w_ref.shape[1]
    tile = pl.program_id(0) % seq_tiles
    buf_ref[0:halo, :] = jnp.where(tile == 0, 0.0, halo_ref[...])
    buf_ref[halo:, :] = v_ref[...]
    pos = tile * ts + lax.broadcasted_iota(I32, (ts, 1), 0)
    for g, win in enumerate(POOL_WINDOWS):
        cols = slice(g * group, (g + 1) * group)
        tok = buf_ref[halo:halo + ts, cols]
        acc = tok
        for j in range(1, win):
            acc = acc + buf_ref[halo - j:halo - j + ts, cols]
        cnt = jnp.minimum(pos + 1, win).astype(F32)
        pooled = acc / cnt - tok
        mixed = jnp.dot(pooled.astype(BF16), w_ref[g], preferred_element_type=F32)
        o_ref[:, cols] = (mixed * sc_ref[:, cols]).astype(o_ref.dtype)


def _pool_mixer(p_pool, pool_w, pool_scale, seq, ts):
    n, d_pool = p_pool.shape
    halo = max(POOL_WINDOWS)
    seq_tiles = seq // ts
    return pl.pallas_call(
        functools.partial(_pool_kernel, seq_tiles=seq_tiles, halo=halo),
        grid=(n // ts,),
        in_specs=[pl.BlockSpec((ts, d_pool), lambda r: (r, 0)),
                  pl.BlockSpec((halo, d_pool), lambda r: (jnp.maximum(r * (ts // halo) - 1, 0), 0)),
                  pl.BlockSpec(pool_w.shape, lambda r: (0, 0, 0)),
                  pl.BlockSpec((1, d_pool), lambda r: (0, 0))],
        out_specs=pl.BlockSpec((ts, d_pool), lambda r: (r, 0)),
        out_shape=jax.ShapeDtypeStruct((n, d_pool), BF16),
        scratch_shapes=[pltpu.VMEM((halo + ts, d_pool), F32)],
        compiler_params=_params(1),
    )(p_pool, p_pool, pool_w, pool_scale)


def _bias_kernel(rb_ref, o_ref, *, tq):
    n_heads, n_slabs, ks, _ = o_ref.shape
    max_exact = N_BUCKETS // 2
    for j in range(n_slabs):
        dist = (lax.broadcasted_iota(I32, (ks, tq), 1) + (tq - ks * j)
                - lax.broadcasted_iota(I32, (ks, tq), 0))
        dist = jnp.maximum(dist, 0)
        d = jnp.maximum(dist, 1).astype(F32)
        large = max_exact + (jnp.log(d / max_exact) / math.log(MAX_DISTANCE / max_exact)
                             * (N_BUCKETS - max_exact)).astype(I32)
        large = jnp.minimum(large, N_BUCKETS - 1)
        bucket = jnp.where(dist < max_exact, dist, large)
        for h in range(n_heads):
            def pick(b, acc, h=h, bucket=bucket):
                return jnp.where(bucket == b, rb_ref[b, h], acc)
            bias = lax.fori_loop(0, N_BUCKETS, pick, jnp.zeros((ks, tq), F32))
            o_ref[h, j] = bias - rb_ref[N_BUCKETS - 1, h]


def _bias_tiles(rel_bias, tq):
    n_heads = rel_bias.shape[1]
    n_slabs = 2 * tq // LANES
    return pl.pallas_call(
        functools.partial(_bias_kernel, tq=tq),
        in_specs=[pl.BlockSpec(memory_space=pltpu.SMEM)],
        out_specs=pl.BlockSpec(memory_space=pltpu.VMEM),
        out_shape=jax.ShapeDtypeStruct((n_heads, n_slabs, LANES, tq), F32),
        compiler_params=pltpu.CompilerParams(vmem_limit_bytes=VMEM_LIMIT),
    )(rel_bias)


def _attn_kernel(qT_ref, qiT_ref, wiT_ref, k_ref, ki2_ref, vT_ref, bias_ref, o_ref, key_ref,
                 *, top_k, tq, hd):
    ks = LANES
    slabs_per_blk = tq // ks
    qb = pl.program_id(1)
    n_slabs = (qb + 1) * slabs_per_blk
    q_pos = qb * tq + lax.broadcasted_iota(I32, (1, tq), 1)

    def key_pos(slab):
        return slab * ks + lax.broadcasted_iota(I32, (ks, 1), 0)

    w_all = wiT_ref[...]

    def score_slab(slab, carry):
        r0 = pl.multiple_of(slab * ks, ks)
        acc = jnp.zeros((ks, tq), F32)
        for p in range(IDX_HEADS // 2):
            qp = qiT_ref[p * LANES:(p + 1) * LANES, :]
            z0 = jnp.dot(ki2_ref[pl.ds(r0, ks), 0:LANES], qp, preferred_element_type=F32)
            z1 = jnp.dot(ki2_ref[pl.ds(r0, ks), LANES:2 * LANES], qp, preferred_element_type=F32)
            acc = acc + w_all[2 * p:2 * p + 1, :] * jnp.maximum(z0, 0.0)
            acc = acc + w_all[2 * p + 1:2 * p + 2, :] * jnp.maximum(z1, 0.0)
        acc = jnp.where(key_pos(slab) <= q_pos, acc, NEG_INF)
        bits = pltpu.bitcast(acc, I32)
        key_ref[pl.ds(r0, ks), :] = bits ^ ((bits >> 31) & 0x7FFFFFFF)
        return carry

    lax.fori_loop(0, n_slabs, score_slab, 0)

    def bit_step(i, prefix):
        cand = prefix | lax.shift_left(jnp.int32(1), 31 - i)
        cand_s = cand ^ INT_MIN

        def count_slab(slab, c8):
            r0 = pl.multiple_of(slab * ks, ks)
            hit = (key_ref[pl.ds(r0, ks), :] >= cand_s).astype(F32)
            return c8 + jnp.sum(hit.reshape(ks // SUBLANES, SUBLANES, tq), axis=0)

        c8 = lax.fori_loop(0, n_slabs, count_slab, jnp.zeros((SUBLANES, tq), F32))
        cnt = jnp.sum(c8, axis=0, keepdims=True)
        return jnp.where(cnt >= top_k, cand, prefix)

    prefix = lax.fori_loop(0, 32, bit_step, jnp.zeros((1, tq), I32))
    thr = prefix ^ INT_MIN

    def mask_slab(slab, carry):
        r0 = pl.multiple_of(slab * ks, ks)
        sel = (key_ref[pl.ds(r0, ks), :] >= thr) & (key_pos(slab) <= q_pos)
        key_ref[pl.ds(r0, ks), :] = pltpu.bitcast(jnp.where(sel, 0.0, NEG_INF).astype(F32), I32)
        return carry

    lax.fori_loop(0, n_slabs, mask_slab, 0)

    n_near = 2 * slabs_per_blk
    near0 = jnp.maximum(n_slabs - n_near, 0)
    bias0 = n_near - (n_slabs - near0)

    for h in range(N_HEADS):
        hs = slice(h * hd, (h + 1) * hd)
        qh = qT_ref[hs, :]

        def step(slab, carry, with_bias, h=h, hs=hs, qh=qh):
            m, l, acc = carry
            r0 = pl.multiple_of(slab * ks, ks)
            s = jnp.dot(k_ref[pl.ds(r0, ks), hs], qh, preferred_element_type=F32)
            s = s + pltpu.bitcast(key_ref[pl.ds(r0, ks), :], F32)
            if with_bias:
                s = s + bias_ref[h, slab - near0 + bias0]
            m_new = jnp.maximum(m, jnp.max(s, axis=0, keepdims=True))
            alpha = jnp.exp(m - m_new)
            p = jnp.exp(s - m_new)
            l = alpha * l + jnp.sum(p, axis=0, keepdims=True)
            acc = alpha * acc + jnp.dot(vT_ref[slab, hs, :], p.astype(BF16),
                                        preferred_element_type=F32)
            return m_new, l, acc

        init = (jnp.full((1, tq), M_FLOOR, F32), jnp.zeros((1, tq), F32), jnp.zeros((hd, tq), F32))
        carry = lax.fori_loop(0, near0, functools.partial(step, with_bias=False), init)
        m, l, acc = lax.fori_loop(near0, n_slabs, functools.partial(step, with_bias=True), carry)
        o_ref[hs, :] = (acc / l).astype(o_ref.dtype)


def _layer_norm_t(r, g, b):
    mu = jnp.mean(r, axis=0, keepdims=True)
    c = r - mu
    var = jnp.mean(c * c, axis=0, keepdims=True)
    return c * lax.rsqrt(var + LN_EPS) * g + b


def _outproj_kernel(pool_ref, attnT_ref, x_ref, wpT_ref, waT_ref, g_ref, b_ref, hT_ref, hTb_ref,
                    *, alpha):
    mix = lax.dot_general(wpT_ref[...], pool_ref[...], (((1,), (1,)), ((), ())),
                          preferred_element_type=F32)
    mix = mix + jnp.dot(waT_ref[...], attnT_ref[...], preferred_element_type=F32)
    h = _layer_norm_t(alpha * x_ref[...].T + mix, g_ref[...], b_ref[...])
    hT_ref[...] = h
    hTb_ref[...] = h.astype(BF16)


def _out_projection(pool_out, attnT, x2, wpT, waT, g, b, alpha, tm):
    n, d = x2.shape
    d_pool = pool_out.shape[1]
    d_attn = attnT.shape[0]
    return pl.pallas_call(
        functools.partial(_outproj_kernel, alpha=alpha),
        grid=(n // tm,),
        in_specs=[pl.BlockSpec((tm, d_pool), lambda i: (i, 0)),
                  pl.BlockSpec((d_attn, tm), lambda i: (0, i)),
                  pl.BlockSpec((tm, d), lambda i: (i, 0)),
                  _resident((d, d_pool), lambda i: (0, 0)),
                  _resident((d, d_attn), lambda i: (0, 0)),
                  _resident((d, 1), lambda i: (0, 0)),
                  _resident((d, 1), lambda i: (0, 0))],
        out_specs=[pl.BlockSpec((d, tm), lambda i: (0, i)),
                   pl.BlockSpec((d, tm), lambda i: (0, i))],
        out_shape=[jax.ShapeDtypeStruct((d, n), F32), jax.ShapeDtypeStruct((d, n), BF16)],
        compiler_params=_params(1),
    )(pool_out, attnT, x2, wpT, waT, g, b)


def _peer_query_kernel(hTb_ref, wqT_ref, sk_ref, sT_ref):
    qhT = jnp.dot(wqT_ref[...], hTb_ref[...], preferred_element_type=F32).astype(BF16)
    half = sk_ref.shape[2]
    for c in range(sT_ref.shape[0]):
        sT_ref[c] = jnp.dot(sk_ref[c % 2], qhT[c * half:(c + 1) * half, :],
                            preferred_element_type=F32)


def _peer_query(hTb, wqT, sub_keys, tm):
    d, n = hTb.shape
    dq = wqT.shape[0]
    n_keys, half = sub_keys.shape[1:]
    n_slices = dq // half
    return pl.pallas_call(
        _peer_query_kernel,
        grid=(n // tm,),
        in_specs=[pl.BlockSpec((d, tm), lambda i: (0, i)),
                  _resident((dq, d), lambda i: (0, 0)),
                  _resident(sub_keys.shape, lambda i: (0, 0, 0))],
        out_specs=pl.BlockSpec((n_slices, n_keys, tm), lambda i: (0, 0, i)),
        out_shape=jax.ShapeDtypeStruct((n_slices, n_keys, n), F32),
        compiler_params=_params(1),
    )(hTb, wqT, sub_keys)


def _bitonic_sort_desc(xs):
    xs = list(xs)
    n = len(xs)
    k = 2
    while k <= n:
        j = k // 2
        while j >= 1:
            for i in range(n):
                o = i ^ j
                if o > i:
                    hi, lo = jnp.maximum(xs[i], xs[o]), jnp.minimum(xs[i], xs[o])
                    xs[i], xs[o] = (hi, lo) if (i & k) == 0 else (lo, hi)
            j //= 2
        k *= 2
    return xs


def _bitonic_merge_desc(xs):
    xs = list(xs)
    n = len(xs)
    j = n // 2
    while j >= 1:
        for i in range(n):
            o = i ^ j
            if o > i:
                xs[i], xs[o] = jnp.maximum(xs[i], xs[o]), jnp.minimum(xs[i], xs[o])
        j //= 2
    return xs


def _top_of_two_sorted(a, b):
    n = len(a)
    return [jnp.maximum(a[i], b[n - 1 - i]) for i in range(n)]


def _top16_over_keys(s):
    groups = s.shape[0] // SUBLANES
    xs = _bitonic_sort_desc([s[g * SUBLANES:(g + 1) * SUBLANES, :] for g in range(groups)])
    shift = SUBLANES // 2
    while shift >= 1:
        other = [pltpu.roll(a, shift, 0) for a in xs]
        xs = _bitonic_merge_desc(_top_of_two_sorted(xs, other))
        shift //= 2
    return xs


def _peer_route_kernel(sT_ref, thr_ref, cc_ref):
    k = PEER_TOPK
    for h in range(PEER_HEADS):
        a = _top16_over_keys(sT_ref[2 * h])
        b = _top16_over_keys(sT_ref[2 * h + 1])
        cands = [a[i] + b[j] for i in range(k) for j in range(k) if (i + 1) * (j + 1) <= k]
        pad = jnp.full_like(cands[0], -jnp.inf)
        cands = cands + [pad] * (-len(cands) % k)
        top = _bitonic_sort_desc(cands[:k])
        for g in range(1, len(cands) // k):
            nxt = _bitonic_sort_desc(cands[g * k:(g + 1) * k])
            top = _top_of_two_sorted(top, nxt)
            if g + 1 < len(cands) // k:
                top = _bitonic_merge_desc(top)
        cmax = a[0] + b[0]
        thr = functools.reduce(jnp.minimum, top)
        z = functools.reduce(jnp.add, [jnp.exp(t - cmax) for t in top])
        thr_ref[h:h + 1, :] = thr[0:1, :]
        cc_ref[h:h + 1, :] = (-cmax - jnp.log(z))[0:1, :]


def _peer_route(sT, tt):
    n_slices, n_keys, n = sT.shape
    small = pl.BlockSpec((PEER_HEADS, tt), lambda i: (0, i))
    return pl.pallas_call(
        _peer_route_kernel,
        grid=(n // tt,),
        in_specs=[pl.BlockSpec((n_slices, n_keys, tt), lambda i: (0, 0, i))],
        out_specs=[small, small],
        out_shape=[jax.ShapeDtypeStruct((PEER_HEADS, n), F32)] * 2,
        compiler_params=_params(1),
    )(sT)


def _peer_dense_kernel(hTb_ref, u_ref, vT_ref, sT_ref, thr_ref, cc_ref, o_ref):
    ei = pl.program_id(1)
    te = u_ref.shape[0]
    rows = te // PEER_KEYS

    @pl.when(ei == 0)
    def _():
        o_ref[...] = jnp.zeros_like(o_ref)

    act = jnp.dot(u_ref[...], hTb_ref[...], preferred_element_type=F32)
    act = 0.5 * act * (1.0 + lax.erf(act * math.sqrt(0.5)))
    i0 = pl.multiple_of(ei * rows, rows)
    parts = []
    for r in range(rows):
        gate = jnp.zeros((PEER_KEYS, act.shape[1]), F32)
        for h in range(PEER_HEADS):
            s1 = sT_ref[2 * h, pl.ds(i0, rows), :][r:r + 1, :]
            s = s1 + sT_ref[2 * h + 1]
            gate = gate + jnp.where(s >= thr_ref[h:h + 1, :], jnp.exp(s + cc_ref[h:h + 1, :]), 0.0)
        parts.append((gate * act[r * PEER_KEYS:(r + 1) * PEER_KEYS, :]).astype(BF16))
    p = jnp.concatenate(parts, axis=0)
    o_ref[...] += jnp.dot(vT_ref[...], p, preferred_element_type=F32)


def _peer_dense(hTb, u, vT, sT, thr, cc, tt, te):
    d, n = hTb.shape
    n_exp = u.shape[0]
    n_slices, n_keys, _ = sT.shape
    small = pl.BlockSpec((PEER_HEADS, tt), lambda t, e: (0, t))
    return pl.pallas_call(
        _peer_dense_kernel,
        grid=(n // tt, n_exp // te),
        in_specs=[pl.BlockSpec((d, tt), lambda t, e: (0, t)),
                  pl.BlockSpec((te, d), lambda t, e: (e, 0)),
                  pl.BlockSpec((d, te), lambda t, e: (0, e)),
                  pl.BlockSpec((n_slices, n_keys, tt), lambda t, e: (0, 0, t)),
                  small, small],
        out_specs=pl.BlockSpec((d, tt), lambda t, e: (0, t)),
        out_shape=jax.ShapeDtypeStruct((d, n), F32),
        compiler_params=_params(2),
    )(hTb, u, vT, sT, thr, cc)


def _final_ln_kernel(hT_ref, fT_ref, g_ref, b_ref, o_ref, *, alpha):
    y = _layer_norm_t(alpha * hT_ref[...] + fT_ref[...], g_ref[...], b_ref[...])
    o_ref[...] = y.T


def _final_ln(hT, ffnT, g, b, alpha, tm):
    d, n = hT.shape
    return pl.pallas_call(
        functools.partial(_final_ln_kernel, alpha=alpha),
        grid=(n // tm,),
        in_specs=[pl.BlockSpec((d, tm), lambda i: (0, i)),
                  pl.BlockSpec((d, tm), lambda i: (0, i)),
                  _resident((d, 1), lambda i: (0, 0)),
                  _resident((d, 1), lambda i: (0, 0))],
        out_specs=pl.BlockSpec((tm, d), lambda i: (i, 0)),
        out_shape=jax.ShapeDtypeStruct((n, d), F32),
        compiler_params=_params(1),
    )(hT, ffnT, g, b)


def _tile(n, pref):
    t = min(pref, n)
    while n % t:
        t -= LANES
    return t


def _layer(h2, batch, seq, w_in, pool_w, pool_scale, bias, w_out, ln1_g, ln1_b,
           peer_wq, sub_keys, peer_u, peer_v, ln2_g, ln2_b, alpha, tq):
    n, d = h2.shape
    d_pool = pool_w.shape[0] * pool_w.shape[1]
    d_attn = w_out.shape[0] - d_pool
    hd = d_attn // N_HEADS
    d_qi = IDX_HEADS * IDX_DIM
    top_k = min(MAX_TOPK, seq // 4)

    o = 0
    w_pool = w_in[:, o:o + d_pool]; o += d_pool
    w_q = w_in[:, o:o + d_attn] * hd ** -0.5; o += d_attn
    w_k = w_in[:, o:o + d_attn]; o += d_attn
    w_v = w_in[:, o:o + d_attn]; o += d_attn
    w_qi = w_in[:, o:o + d_qi]; o += d_qi
    w_ki = w_in[:, o:o + IDX_DIM]; o += IDX_DIM
    w_wi = w_in[:, o:o + IDX_HEADS] * (IDX_HEADS * IDX_DIM) ** -0.5
    zeros = jnp.zeros_like(w_ki)
    w_nat = jnp.concatenate([w_k, w_ki, zeros, zeros, w_ki], axis=1).astype(BF16)
    w_qT = jnp.concatenate([w_q.T, w_qi.T], axis=0).astype(BF16)

    x_bf = h2.astype(BF16)
    tm = _tile(n, 1024)
    p_pool = _matmul(x_bf, w_pool.astype(BF16), F32, tm, _tile(d_pool, 512))
    k_nat = _matmul(x_bf, w_nat, BF16, tm, w_nat.shape[1])
    qT_all = _matmul_nt(w_qT, x_bf, BF16, _tile(w_qT.shape[0], 1024), tm)
    vT_blk = _matmul_nt_blocked(w_v.T.astype(BF16), x_bf, BF16, _tile(seq, 512))
    wiT = _matmul_nt(w_wi.T.astype(BF16), x_bf, F32, IDX_HEADS, tm)

    pool_out = _pool_mixer(p_pool, pool_w.astype(BF16), pool_scale.reshape(1, d_pool), seq,
                           _tile(seq, 256))

    attnT = _sparse_attention_split(qT_all, wiT, k_nat, vT_blk, bias, batch, seq, tq, top_k, d_attn)

    w_pT = w_out[:d_pool].T.astype(BF16)
    w_aT = w_out[d_pool:].T.astype(BF16)
    hT, hTb = _out_projection(pool_out, attnT, h2, w_pT, w_aT, ln1_g.reshape(d, 1),
                              ln1_b.reshape(d, 1), alpha, _tile(n, 256))

    sT = _peer_query(hTb, peer_wq.T.astype(BF16), sub_keys.astype(BF16), _tile(n, 256))
    thr, cc = _peer_route(sT, _tile(n, 256))
    ffnT = _peer_dense(hTb, peer_u.astype(BF16), peer_v.T.astype(BF16), sT, thr, cc,
                       _tile(n, 512), 8 * PEER_KEYS)
    return _final_ln(hT, ffnT, ln2_g.reshape(d, 1), ln2_b.reshape(d, 1), alpha, _tile(n, 256))


def _sparse_attention_split(qT_all, wiT, k_nat, vT_blk, bias, batch, seq, tq, top_k, d_attn):
    n = batch * seq
    nq = seq // tq
    hd = d_attn // N_HEADS
    slabs = seq // LANES
    d_qi = IDX_HEADS * IDX_DIM
    assert d_qi == d_attn, "q and indexer-q row blocks must have equal height"
    tok0 = lambda b, q: (0, b * nq + q)
    tok1 = lambda b, q: (1, b * nq + q)
    return pl.pallas_call(
        functools.partial(_attn_kernel, top_k=top_k, tq=tq, hd=hd),
        grid=(batch, nq),
        in_specs=[pl.BlockSpec((d_attn, tq), tok0),
                  pl.BlockSpec((d_qi, tq), tok1),
                  pl.BlockSpec((IDX_HEADS, tq), tok0),
                  _resident((seq, d_attn), lambda b, q: (b, 0)),
                  _resident((seq, 4 * IDX_DIM), lambda b, q: (b, d_attn // (4 * IDX_DIM))),
                  _resident((slabs, d_attn, LANES), lambda b, q: (b, 0, 0)),
                  _resident(bias.shape, lambda b, q: (0, 0, 0, 0))],
        out_specs=pl.BlockSpec((d_attn, tq), tok0),
        out_shape=jax.ShapeDtypeStruct((d_attn, n), BF16),
        scratch_shapes=[pltpu.VMEM((seq, tq), I32)],
        compiler_params=_params(2),
    )(qT_all, qT_all, wiT, k_nat, k_nat, vT_blk, bias)


def kernel(x, w_in, pool_w, pool_scale, rel_bias, w_out, ln1_g, ln1_b, peer_wq, peer_subkeys,
           peer_u, peer_v, ln2_g, ln2_b):
    batch, seq, d = x.shape
    depth = w_in.shape[0]
    alpha = (2 * depth) ** 0.25
    tq = min(256, seq)
    bias = _bias_tiles(rel_bias, tq)
    h2 = x.reshape(batch * seq, d)
    for l in range(depth):
        h2 = _layer(h2, batch, seq, w_in[l], pool_w[l], pool_scale[l], bias, w_out[l],
                    ln1_g[l], ln1_b[l], peer_wq[l], peer_subkeys[l], peer_u[l], peer_v[l],
                    ln2_g[l], ln2_b[l], alpha, tq)
    return h2.reshape(batch, seq, d)
```

```python
import functools
import math

import jax
import jax.numpy as jnp
from jax import lax
from jax.experimental import pallas as pl
from jax.experimental.pallas import tpu as pltpu

F32 = jnp.float32
BF16 = jnp.bfloat16
I32 = jnp.int32

LANES = 128
SUBLANES = 8
VMEM_LIMIT = 56 * 1024 * 1024

POOL_WINDOWS = (2, 4, 8, 16)
N_HEADS = 8
IDX_HEADS = 16
IDX_DIM = 64
MAX_TOPK = 256
N_BUCKETS = 32
MAX_DISTANCE = 128
PEER_HEADS = 8
PEER_KEYS = 128
PEER_TOPK = 16
LN_EPS = 1e-5
NEG_INF = -1e30
M_FLOOR = -1e20
INT_MIN = -(2 ** 31)
LOG2E = 1.0 / math.log(2.0)


def _params(n_grid):
    return pltpu.CompilerParams(dimension_semantics=("arbitrary",) * n_grid,
                                vmem_limit_bytes=VMEM_LIMIT)


def _resident(shape, index_map):
    return pl.BlockSpec(shape, index_map, pipeline_mode=pl.Buffered(1))


def _mm_kernel(a_ref, b_ref, o_ref):
    o_ref[...] = jnp.dot(a_ref[...], b_ref[...], preferred_element_type=F32).astype(o_ref.dtype)


def _matmul(a, b, out_dtype, tm, tn):
    m, k = a.shape
    n = b.shape[1]
    return pl.pallas_call(
        _mm_kernel,
        grid=(n // tn, m // tm),
        in_specs=[pl.BlockSpec((tm, k), lambda j, i: (i, 0)),
                  pl.BlockSpec((k, tn), lambda j, i: (0, j))],
        out_specs=pl.BlockSpec((tm, tn), lambda j, i: (i, j)),
        out_shape=jax.ShapeDtypeStruct((m, n), out_dtype),
        compiler_params=_params(2),
    )(a, b)


def _mm_nt_kernel(w_ref, x_ref, o_ref):
    o_ref[...] = lax.dot_general(w_ref[...], x_ref[...], (((1,), (1,)), ((), ())),
                                 preferred_element_type=F32).astype(o_ref.dtype)


def _matmul_nt(w_t, x, out_dtype, tr, tm):
    r, k = w_t.shape
    m = x.shape[0]
    return pl.pallas_call(
        _mm_nt_kernel,
        grid=(r // tr, m // tm),
        in_specs=[pl.BlockSpec((tr, k), lambda j, i: (j, 0)),
                  pl.BlockSpec((tm, k), lambda j, i: (i, 0))],
        out_specs=pl.BlockSpec((tr, tm), lambda j, i: (j, i)),
        out_shape=jax.ShapeDtypeStruct((r, m), out_dtype),
        compiler_params=_params(2),
    )(w_t, x)


def _mm_nt_blocked_kernel(w_ref, x_ref, o_ref):
    res = lax.dot_general(w_ref[...], x_ref[...], (((1,), (1,)), ((), ())),
                          preferred_element_type=F32).astype(o_ref.dtype)
    slab = o_ref.shape[2]
    for j in range(o_ref.shape[0]):
        o_ref[j] = res[:, j * slab:(j + 1) * slab]


def _matmul_nt_blocked(w_t, x, out_dtype, tm, slab):
    r, k = w_t.shape
    m = x.shape[0]
    return pl.pallas_call(
        _mm_nt_blocked_kernel,
        grid=(m // tm,),
        in_specs=[pl.BlockSpec((r, k), lambda i: (0, 0)),
                  pl.BlockSpec((tm, k), lambda i: (i, 0))],
        out_specs=pl.BlockSpec((tm // slab, r, slab), lambda i: (i, 0, 0)),
        out_shape=jax.ShapeDtypeStruct((m // slab, r, slab), out_dtype),
        compiler_params=_params(1),
    )(w_t, x)


def _pool_kernel(v_ref, halo_ref, w_ref, sc_ref, o_ref, buf_ref, *, seq_tiles, halo):
    ts = v_ref.shape[0]
    group = w_ref.shape[1]
    tile = pl.program_id(0) % seq_tiles
    buf_ref[0:halo, :] = jnp.where(tile == 0, 0.0, halo_ref[...])
    buf_ref[halo:, :] = v_ref[...]
    pos = tile * ts + lax.broadcasted_iota(I32, (ts, 1), 0)
    for g, win in enumerate(POOL_WINDOWS):
        cols = slice(g * group, (g + 1) * group)
        tok = buf_ref[halo:halo + ts, cols]
        acc = tok
        for j in range(1, win):
            acc = acc + buf_ref[halo - j:halo - j + ts, cols]
        cnt = jnp.minimum(pos + 1, win).astype(F32)
        pooled = acc / cnt - tok
        mixed = jnp.dot(pooled.astype(BF16), w_ref[g], preferred_element_type=F32)
        o_ref[:, cols] = (mixed * sc_ref[:, cols]).astype(o_ref.dtype)


def _pool_mixer(p_pool, pool_w, pool_scale, seq, ts):
    n, d_pool = p_pool.shape
    halo = max(POOL_WINDOWS)
    seq_tiles = seq // ts
    return pl.pallas_call(
        functools.partial(_pool_kernel, seq_tiles=seq_tiles, halo=halo),
        grid=(n // ts,),
        in_specs=[pl.BlockSpec((ts, d_pool), lambda r: (r, 0)),
                  pl.BlockSpec((halo, d_pool), lambda r: (jnp.maximum(r * (ts // halo) - 1, 0), 0)),
                  pl.BlockSpec(pool_w.shape, lambda r: (0, 0, 0)),
                  pl.BlockSpec((1, d_pool), lambda r: (0, 0))],
        out_specs=pl.BlockSpec((ts, d_pool), lambda r: (r, 0)),
        out_shape=jax.ShapeDtypeStruct((n, d_pool), BF16),
        scratch_shapes=[pltpu.VMEM((halo + ts, d_pool), F32)],
        compiler_params=_params(1),
    )(p_pool, p_pool, pool_w, pool_scale)


def _bias_kernel(rb_ref, o_ref, *, tq):
    n_heads, n_slabs, ks, _ = o_ref.shape
    max_exact = N_BUCKETS // 2
    for j in range(n_slabs):
        dist = (lax.broadcasted_iota(I32, (ks, tq), 1) + (tq - ks * j)
                - lax.broadcasted_iota(I32, (ks, tq), 0))
        dist = jnp.maximum(dist, 0)
        d = jnp.maximum(dist, 1).astype(F32)
        large = max_exact + (jnp.log(d / max_exact) / math.log(MAX_DISTANCE / max_exact)
                             * (N_BUCKETS - max_exact)).astype(I32)
        large = jnp.minimum(large, N_BUCKETS - 1)
        bucket = jnp.where(dist < max_exact, dist, large)
        for h in range(n_heads):
            def pick(b, acc, h=h, bucket=bucket):
                return jnp.where(bucket == b, rb_ref[b, h], acc)
            bias = lax.fori_loop(0, N_BUCKETS, pick, jnp.zeros((ks, tq), F32))
            o_ref[h, j] = bias - rb_ref[N_BUCKETS - 1, h]


def _bias_tiles(rel_bias, tq):
    n_heads = rel_bias.shape[1]
    return pl.pallas_call(
        functools.partial(_bias_kernel, tq=tq),
        in_specs=[pl.BlockSpec(memory_space=pltpu.SMEM)],
        out_specs=pl.BlockSpec(memory_space=pltpu.VMEM),
        out_shape=jax.ShapeDtypeStruct((n_heads, 2, tq, tq), F32),
        compiler_params=pltpu.CompilerParams(vmem_limit_bytes=VMEM_LIMIT),
    )(rel_bias)


def _attn_kernel(qT_ref, qiT_ref, wiT_ref, k_ref, ki2_ref, vT_ref, bias_ref, o_ref,
                 key_ref, m_ref, l_ref, acc_ref, *, top_k, tq, hd):
    ks = LANES
    halves = tq // ks
    qb = pl.program_id(1)
    n_blk = qb + 1
    q_pos = qb * tq + lax.broadcasted_iota(I32, (1, tq), 1)

    def key_pos(r0):
        return r0 + lax.broadcasted_iota(I32, (ks, 1), 0)

    w_all = wiT_ref[...]

    def score_blk(kb, carry):
        for hf in range(halves):
            r0 = pl.multiple_of(kb * tq + hf * ks, ks)
            acc = jnp.zeros((ks, tq), F32)
            for p in range(IDX_HEADS // 2):
                qp = qiT_ref[p * LANES:(p + 1) * LANES, :]
                z0 = jnp.dot(ki2_ref[pl.ds(r0, ks), 0:LANES], qp, preferred_element_type=F32)
                z1 = jnp.dot(ki2_ref[pl.ds(r0, ks), LANES:2 * LANES], qp,
                             preferred_element_type=F32)
                acc = acc + w_all[2 * p:2 * p + 1, :] * jnp.maximum(z0, 0.0)
                acc = acc + w_all[2 * p + 1:2 * p + 2, :] * jnp.maximum(z1, 0.0)
            acc = jnp.where(key_pos(r0) <= q_pos, acc, NEG_INF)
            bits = pltpu.bitcast(acc, I32)
            key_ref[pl.ds(r0, ks), :] = bits ^ ((bits >> 31) & 0x7FFFFFFF)
        return carry

    lax.fori_loop(0, n_blk, score_blk, 0)

    def bit_step(i, prefix):
        cand = prefix | lax.shift_left(jnp.int32(1), 31 - i)
        cand_s = cand ^ INT_MIN

        def count_blk(kb, c8):
            for hf in range(halves):
                r0 = pl.multiple_of(kb * tq + hf * ks, ks)
                hit = (key_ref[pl.ds(r0, ks), :] >= cand_s).astype(F32)
                c8 = c8 + jnp.sum(hit.reshape(ks // SUBLANES, SUBLANES, tq), axis=0)
            return c8

        c8 = lax.fori_loop(0, n_blk, count_blk, jnp.zeros((SUBLANES, tq), F32))
        cnt = jnp.sum(c8, axis=0, keepdims=True)
        return jnp.where(cnt >= top_k, cand, prefix)

    prefix = lax.fori_loop(0, 32, bit_step, jnp.zeros((1, tq), I32))
    thr = prefix ^ INT_MIN

    def mask_blk(kb, carry):
        for hf in range(halves):
            r0 = pl.multiple_of(kb * tq + hf * ks, ks)
            sel = (key_ref[pl.ds(r0, ks), :] >= thr) & (key_pos(r0) <= q_pos)
            key_ref[pl.ds(r0, ks), :] = pltpu.bitcast(jnp.where(sel, 0.0, NEG_INF).astype(F32), I32)
        return carry

    lax.fori_loop(0, n_blk, mask_blk, 0)

    m_ref[...] = jnp.full(m_ref.shape, M_FLOOR, F32)
    l_ref[...] = jnp.zeros(l_ref.shape, F32)
    acc_ref[...] = jnp.zeros(acc_ref.shape, F32)

    def step(kb, carry, with_bias):
        r0 = pl.multiple_of(kb * tq, tq)
        mask = pltpu.bitcast(key_ref[pl.ds(r0, tq), :], F32)

        def logits(h):
            hs = slice(h * hd, (h + 1) * hd)
            return jnp.dot(k_ref[pl.ds(r0, tq), hs], qT_ref[hs, :], preferred_element_type=F32)

        s_next = logits(0)
        for h in range(N_HEADS):
            hs = slice(h * hd, (h + 1) * hd)
            s = s_next + mask
            if h + 1 < N_HEADS:
                s_next = logits(h + 1)
            if with_bias:
                s = s + bias_ref[h, kb - (qb - 1)]
            m = m_ref[h]
            m_new = jnp.maximum(m, jnp.max(s, axis=0, keepdims=True))
            alpha = jnp.exp(m - m_new)
            p = jnp.exp(s - m_new)
            l_ref[h] = alpha * l_ref[h] + jnp.sum(p, axis=0, keepdims=True)
            acc_ref[h] = alpha * acc_ref[h] + jnp.dot(vT_ref[kb, hs, :], p.astype(BF16),
                                                      preferred_element_type=F32)
            m_ref[h] = m_new
        return carry

    near0 = jnp.maximum(qb - 1, 0)
    lax.fori_loop(0, near0, functools.partial(step, with_bias=False), 0)
    lax.fori_loop(near0, n_blk, functools.partial(step, with_bias=True), 0)
    for h in range(N_HEADS):
        o_ref[h * hd:(h + 1) * hd, :] = (acc_ref[h] / l_ref[h]).astype(o_ref.dtype)


def _layer_norm_t(r, g, b):
    mu = jnp.mean(r, axis=0, keepdims=True)
    c = r - mu
    var = jnp.mean(c * c, axis=0, keepdims=True)
    return c * lax.rsqrt(var + LN_EPS) * g + b


def _outproj_kernel(pool_ref, attnT_ref, x_ref, wpT_ref, waT_ref, g_ref, b_ref, hT_ref, hTb_ref,
                    *, alpha):
    mix = lax.dot_general(wpT_ref[...], pool_ref[...], (((1,), (1,)), ((), ())),
                          preferred_element_type=F32)
    mix = mix + jnp.dot(waT_ref[...], attnT_ref[...], preferred_element_type=F32)
    h = _layer_norm_t(alpha * x_ref[...].T + mix, g_ref[...], b_ref[...])
    hT_ref[...] = h
    hTb_ref[...] = h.astype(BF16)


def _out_projection(pool_out, attnT, x2, wpT, waT, g, b, alpha, tm):
    n, d = x2.shape
    d_pool = pool_out.shape[1]
    d_attn = attnT.shape[0]
    return pl.pallas_call(
        functools.partial(_outproj_kernel, alpha=alpha),
        grid=(n // tm,),
        in_specs=[pl.BlockSpec((tm, d_pool), lambda i: (i, 0)),
                  pl.BlockSpec((d_attn, tm), lambda i: (0, i)),
                  pl.BlockSpec((tm, d), lambda i: (i, 0)),
                  _resident((d, d_pool), lambda i: (0, 0)),
                  _resident((d, d_attn), lambda i: (0, 0)),
                  _resident((d, 1), lambda i: (0, 0)),
                  _resident((d, 1), lambda i: (0, 0))],
        out_specs=[pl.BlockSpec((d, tm), lambda i: (0, i)),
                   pl.BlockSpec((d, tm), lambda i: (0, i))],
        out_shape=[jax.ShapeDtypeStruct((d, n), F32), jax.ShapeDtypeStruct((d, n), BF16)],
        compiler_params=_params(1),
    )(pool_out, attnT, x2, wpT, waT, g, b)


def _peer_query_kernel(hTb_ref, wqT_ref, sk_ref, sT_ref):
    qhT = jnp.dot(wqT_ref[...], hTb_ref[...], preferred_element_type=F32).astype(BF16)
    half = sk_ref.shape[2]
    for c in range(sT_ref.shape[0]):
        sT_ref[c] = jnp.dot(sk_ref[c % 2], qhT[c * half:(c + 1) * half, :],
                            preferred_element_type=F32)


def _peer_query(hTb, wqT, sub_keys, tm):
    d, n = hTb.shape
    dq = wqT.shape[0]
    n_keys, half = sub_keys.shape[1:]
    n_slices = dq // half
    return pl.pallas_call(
        _peer_query_kernel,
        grid=(n // tm,),
        in_specs=[pl.BlockSpec((d, tm), lambda i: (0, i)),
                  _resident((dq, d), lambda i: (0, 0)),
                  _resident(sub_keys.shape, lambda i: (0, 0, 0))],
        out_specs=pl.BlockSpec((n_slices, n_keys, tm), lambda i: (0, 0, i)),
        out_shape=jax.ShapeDtypeStruct((n_slices, n_keys, n), F32),
        compiler_params=_params(1),
    )(hTb, wqT, sub_keys)


def _bitonic_sort_desc(xs):
    xs = list(xs)
    n = len(xs)
    k = 2
    while k <= n:
        j = k // 2
        while j >= 1:
            for i in range(n):
                o = i ^ j
                if o > i:
                    hi, lo = jnp.maximum(xs[i], xs[o]), jnp.minimum(xs[i], xs[o])
                    xs[i], xs[o] = (hi, lo) if (i & k) == 0 else (lo, hi)
            j //= 2
        k *= 2
    return xs


def _bitonic_merge_desc(xs):
    xs = list(xs)
    n = len(xs)
    j = n // 2
    while j >= 1:
        for i in range(n):
            o = i ^ j
            if o > i:
                xs[i], xs[o] = jnp.maximum(xs[i], xs[o]), jnp.minimum(xs[i], xs[o])
        j //= 2
    return xs


def _top_of_two_sorted(a, b):
    n = len(a)
    return [jnp.maximum(a[i], b[n - 1 - i]) for i in range(n)]


def _top16_over_keys(s):
    groups = s.shape[0] // SUBLANES
    xs = _bitonic_sort_desc([s[g * SUBLANES:(g + 1) * SUBLANES, :] for g in range(groups)])
    shift = SUBLANES // 2
    while shift >= 1:
        other = [pltpu.roll(a, shift, 0) for a in xs]
        xs = _bitonic_merge_desc(_top_of_two_sorted(xs, other))
        shift //= 2
    return xs


def _peer_route_kernel(sT_ref, s1_ref, s2_ref, thr_ref):
    k = PEER_TOPK
    for h in range(PEER_HEADS):
        s1 = sT_ref[2 * h]
        s2 = sT_ref[2 * h + 1]
        a = _top16_over_keys(s1)
        b = _top16_over_keys(s2)
        pairs = [(i, j) for i in range(k) for j in range(k) if (i + 1) * (j + 1) <= k]
        cands = [a[i] + b[j] for i, j in pairs]
        pad = jnp.full_like(cands[0], -jnp.inf)
        padded = cands + [pad] * (-len(cands) % k)
        top = _bitonic_sort_desc(padded[:k])
        for g in range(1, len(padded) // k):
            nxt = _bitonic_sort_desc(padded[g * k:(g + 1) * k])
            top = _top_of_two_sorted(top, nxt)
            if g + 1 < len(padded) // k:
                top = _bitonic_merge_desc(top)
        cmax = a[0] + b[0]
        thr = functools.reduce(jnp.minimum, top)
        z = functools.reduce(jnp.add, [jnp.exp(t - cmax) for t in top])
        cc = -cmax - jnp.log(z)
        a2 = [x * LOG2E for x in a]
        b2 = [(x + cc) * LOG2E for x in b]
        thr2 = functools.reduce(jnp.minimum, [jnp.where(c >= thr, a2[i] + b2[j], jnp.inf)
                                              for c, (i, j) in zip(cands, pairs)])
        s1_ref[h] = s1 * LOG2E
        s2_ref[h] = (s2 + cc[0:1, :]) * LOG2E
        thr_ref[h:h + 1, :] = thr2[0:1, :]


def _peer_route(sT, tt):
    n_slices, n_keys, n = sT.shape
    big = pl.BlockSpec((PEER_HEADS, n_keys, tt), lambda i: (0, 0, i))
    return pl.pallas_call(
        _peer_route_kernel,
        grid=(n // tt,),
        in_specs=[pl.BlockSpec((n_slices, n_keys, tt), lambda i: (0, 0, i))],
        out_specs=[big, big, pl.BlockSpec((PEER_HEADS, tt), lambda i: (0, i))],
        out_shape=[jax.ShapeDtypeStruct((PEER_HEADS, n_keys, n), F32)] * 2
        + [jax.ShapeDtypeStruct((PEER_HEADS, n), F32)],
        compiler_params=_params(1),
    )(sT)


def _peer_dense_kernel(hTb_ref, u_ref, vT_ref, s1_ref, s2_ref, thr_ref, o_ref, *, chunk):
    ei = pl.program_id(1)
    te = u_ref.shape[0]
    rows = te // PEER_KEYS
    rows_per_chunk = chunk // PEER_KEYS

    @pl.when(ei == 0)
    def _():
        o_ref[...] = jnp.zeros_like(o_ref)

    i0 = pl.multiple_of(ei * rows, rows)

    def pre_act(c):
        return jnp.dot(u_ref[c * chunk:(c + 1) * chunk, :], hTb_ref[...],
                       preferred_element_type=F32)

    n_chunks = te // chunk
    act_next = pre_act(0)
    for c in range(n_chunks):
        act = act_next
        if c + 1 < n_chunks:
            act_next = pre_act(c + 1)
        gates = []
        for rr in range(rows_per_chunk):
            r = c * rows_per_chunk + rr
            gate = jnp.zeros((PEER_KEYS, act.shape[1]), F32)
            for h in range(PEER_HEADS):
                t = s1_ref[h, pl.ds(i0, rows), :][r:r + 1, :] + s2_ref[h]
                gate = gate + jnp.where(t >= thr_ref[h:h + 1, :], jnp.exp2(t), 0.0)
            gates.append(gate)
        act = 0.5 * act * (1.0 + lax.erf(act * math.sqrt(0.5)))
        p = jnp.concatenate(
            [(gates[rr] * act[rr * PEER_KEYS:(rr + 1) * PEER_KEYS, :]).astype(BF16)
             for rr in range(rows_per_chunk)], axis=0)
        o_ref[...] += jnp.dot(vT_ref[:, c * chunk:(c + 1) * chunk], p, preferred_element_type=F32)


def _peer_dense(hTb, u, vT, s1, s2, thr, tt, te, chunk):
    d, n = hTb.shape
    n_exp = u.shape[0]
    n_keys = s1.shape[1]
    big = pl.BlockSpec((PEER_HEADS, n_keys, tt), lambda t, e: (0, 0, t))
    return pl.pallas_call(
        functools.partial(_peer_dense_kernel, chunk=chunk),
        grid=(n // tt, n_exp // te),
        in_specs=[pl.BlockSpec((d, tt), lambda t, e: (0, t)),
                  pl.BlockSpec((te, d), lambda t, e: (e, 0)),
                  pl.BlockSpec((d, te), lambda t, e: (0, e)),
                  big, big,
                  pl.BlockSpec((PEER_HEADS, tt), lambda t, e: (0, t))],
        out_specs=pl.BlockSpec((d, tt), lambda t, e: (0, t)),
        out_shape=jax.ShapeDtypeStruct((d, n), F32),
        compiler_params=_params(2),
    )(hTb, u, vT, s1, s2, thr)


def _final_ln_kernel(hT_ref, fT_ref, g_ref, b_ref, o_ref, *, alpha):
    y = _layer_norm_t(alpha * hT_ref[...] + fT_ref[...], g_ref[...], b_ref[...])
    o_ref[...] = y.T


def _final_ln(hT, ffnT, g, b, alpha, tm):
    d, n = hT.shape
    return pl.pallas_call(
        functools.partial(_final_ln_kernel, alpha=alpha),
        grid=(n // tm,),
        in_specs=[pl.BlockSpec((d, tm), lambda i: (0, i)),
                  pl.BlockSpec((d, tm), lambda i: (0, i)),
                  _resident((d, 1), lambda i: (0, 0)),
                  _resident((d, 1), lambda i: (0, 0))],
        out_specs=pl.BlockSpec((tm, d), lambda i: (i, 0)),
        out_shape=jax.ShapeDtypeStruct((n, d), F32),
        compiler_params=_params(1),
    )(hT, ffnT, g, b)


def _tile(n, pref):
    t = min(pref, n)
    while n % t:
        t -= LANES
    return t


def _layer(h2, batch, seq, w_in, pool_w, pool_scale, bias, w_out, ln1_g, ln1_b,
           peer_wq, sub_keys, peer_u, peer_v, ln2_g, ln2_b, alpha, tq):
    n, d = h2.shape
    d_pool = pool_w.shape[0] * pool_w.shape[1]
    d_attn = w_out.shape[0] - d_pool
    hd = d_attn // N_HEADS
    d_qi = IDX_HEADS * IDX_DIM
    top_k = min(MAX_TOPK, seq // 4)

    o = 0
    w_pool = w_in[:, o:o + d_pool]; o += d_pool
    w_q = w_in[:, o:o + d_attn] * hd ** -0.5; o += d_attn
    w_k = w_in[:, o:o + d_attn]; o += d_attn
    w_v = w_in[:, o:o + d_attn]; o += d_attn
    w_qi = w_in[:, o:o + d_qi]; o += d_qi
    w_ki = w_in[:, o:o + IDX_DIM]; o += IDX_DIM
    w_wi = w_in[:, o:o + IDX_HEADS] * (IDX_HEADS * IDX_DIM) ** -0.5
    zeros = jnp.zeros_like(w_ki)
    w_nat = jnp.concatenate([w_k, w_ki, zeros, zeros, w_ki], axis=1).astype(BF16)
    w_qT = jnp.concatenate([w_q.T, w_qi.T], axis=0).astype(BF16)

    x_bf = h2.astype(BF16)
    tm = _tile(n, 1024)
    p_pool = _matmul(x_bf, w_pool.astype(BF16), F32, tm, _tile(d_pool, 512))
    k_nat = _matmul(x_bf, w_nat, BF16, tm, w_nat.shape[1])
    qT_all = _matmul_nt(w_qT, x_bf, BF16, _tile(w_qT.shape[0], 1024), tm)
    vT_blk = _matmul_nt_blocked(w_v.T.astype(BF16), x_bf, BF16, _tile(seq, 512), tq)
    wiT = _matmul_nt(w_wi.T.astype(BF16), x_bf, F32, IDX_HEADS, tm)

    pool_out = _pool_mixer(p_pool, pool_w.astype(BF16), pool_scale.reshape(1, d_pool), seq,
                           _tile(seq, 256))

    attnT = _sparse_attention_split(qT_all, wiT, k_nat, vT_blk, bias, batch, seq, tq, top_k, d_attn)

    w_pT = w_out[:d_pool].T.astype(BF16)
    w_aT = w_out[d_pool:].T.astype(BF16)
    hT, hTb = _out_projection(pool_out, attnT, h2, w_pT, w_aT, ln1_g.reshape(d, 1),
                              ln1_b.reshape(d, 1), alpha, _tile(n, 256))

    sT = _peer_query(hTb, peer_wq.T.astype(BF16), sub_keys.astype(BF16), _tile(n, 256))
    s1, s2, thr = _peer_route(sT, _tile(n, 256))
    ffnT = _peer_dense(hTb, peer_u.astype(BF16), peer_v.T.astype(BF16), s1, s2, thr,
                       _tile(n, 512), 8 * PEER_KEYS, 4 * PEER_KEYS)
    return _final_ln(hT, ffnT, ln2_g.reshape(d, 1), ln2_b.reshape(d, 1), alpha, _tile(n, 256))


def _sparse_attention_split(qT_all, wiT, k_nat, vT_blk, bias, batch, seq, tq, top_k, d_attn):
    n = batch * seq
    nq = seq // tq
    hd = d_attn // N_HEADS
    d_qi = IDX_HEADS * IDX_DIM
    assert d_qi == d_attn, "q and indexer-q row blocks must have equal height"
    tok0 = lambda b, q: (0, b * nq + q)
    tok1 = lambda b, q: (1, b * nq + q)
    return pl.pallas_call(
        functools.partial(_attn_kernel, top_k=top_k, tq=tq, hd=hd),
        grid=(batch, nq),
        in_specs=[pl.BlockSpec((d_attn, tq), tok0),
                  pl.BlockSpec((d_qi, tq), tok1),
                  pl.BlockSpec((IDX_HEADS, tq), tok0),
                  _resident((seq, d_attn), lambda b, q: (b, 0)),
                  _resident((seq, 4 * IDX_DIM), lambda b, q: (b, d_attn // (4 * IDX_DIM))),
                  _resident((nq, d_attn, tq), lambda b, q: (b, 0, 0)),
                  _resident(bias.shape, lambda b, q: (0, 0, 0, 0))],
        out_specs=pl.BlockSpec((d_attn, tq), tok0),
        out_shape=jax.ShapeDtypeStruct((d_attn, n), BF16),
        scratch_shapes=[pltpu.VMEM((seq, tq), I32),
                        pltpu.VMEM((N_HEADS, 1, tq), F32),
                        pltpu.VMEM((N_HEADS, 1, tq), F32),
                        pltpu.VMEM((N_HEADS, hd, tq), F32)],
        compiler_params=_params(2),
    )(qT_all, qT_all, wiT, k_nat, k_nat, vT_blk, bias)


def kernel(x, w_in, pool_w, pool_scale, rel_bias, w_out, ln1_g, ln1_b, peer_wq, peer_subkeys,
           peer_u, peer_v, ln2_g, ln2_b):
    batch, seq, d = x.shape
    depth = w_in.shape[0]
    alpha = (2 * depth) ** 0.25
    tq = min(256, seq)
    bias = _bias_tiles(rel_bias, tq)
    h2 = x.reshape(batch * seq, d)
    for l in range(depth):
        h2 = _layer(h2, batch, seq, w_in[l], pool_w[l], pool_scale[l], bias, w_out[l],
                    ln1_g[l], ln1_b[l], peer_wq[l], peer_subkeys[l], peer_u[l], peer_v[l],
                    ln2_g[l], ln2_b[l], alpha, tq)
    return h2.reshape(batch, seq, d)
```

```python
import functools
import math

import jax
import jax.numpy as jnp
from jax import lax
from jax.experimental import pallas as pl
from jax.experimental.pallas import tpu as pltpu

F32 = jnp.float32
BF16 = jnp.bfloat16
I32 = jnp.int32

LANES = 128
SUBLANES = 8
VMEM_LIMIT = 56 * 1024 * 1024

POOL_WINDOWS = (2, 4, 8, 16)
N_HEADS = 8
IDX_HEADS = 16
IDX_DIM = 64
MAX_TOPK = 256
N_BUCKETS = 32
MAX_DISTANCE = 128
PEER_HEADS = 8
PEER_KEYS = 128
PEER_TOPK = 16
LN_EPS = 1e-5
NEG_INF = -1e30
M_FLOOR = -1e20
INT_MIN = -(2 ** 31)
UNCHECKED_BITS = 20
LOG2E = 1.0 / math.log(2.0)


def _params(n_grid):
    return pltpu.CompilerParams(dimension_semantics=("arbitrary",) * n_grid,
                                vmem_limit_bytes=VMEM_LIMIT)


def _resident(shape, index_map):
    return pl.BlockSpec(shape, index_map, pipeline_mode=pl.Buffered(1))


def _mm_kernel(a_ref, b_ref, o_ref):
    o_ref[...] = jnp.dot(a_ref[...], b_ref[...], preferred_element_type=F32).astype(o_ref.dtype)


def _matmul(a, b, out_dtype, tm, tn):
    m, k = a.shape
    n = b.shape[1]
    return pl.pallas_call(
        _mm_kernel,
        grid=(n // tn, m // tm),
        in_specs=[pl.BlockSpec((tm, k), lambda j, i: (i, 0)),
                  pl.BlockSpec((k, tn), lambda j, i: (0, j))],
        out_specs=pl.BlockSpec((tm, tn), lambda j, i: (i, j)),
        out_shape=jax.ShapeDtypeStruct((m, n), out_dtype),
        compiler_params=_params(2),
    )(a, b)


def _mm_nt_kernel(w_ref, x_ref, o_ref):
    o_ref[...] = lax.dot_general(w_ref[...], x_ref[...], (((1,), (1,)), ((), ())),
                                 preferred_element_type=F32).astype(o_ref.dtype)


def _matmul_nt(w_t, x, out_dtype, tr, tm):
    r, k = w_t.shape
    m = x.shape[0]
    return pl.pallas_call(
        _mm_nt_kernel,
        grid=(r // tr, m // tm),
        in_specs=[pl.BlockSpec((tr, k), lambda j, i: (j, 0)),
                  pl.BlockSpec((tm, k), lambda j, i: (i, 0))],
        out_specs=pl.BlockSpec((tr, tm), lambda j, i: (j, i)),
        out_shape=jax.ShapeDtypeStruct((r, m), out_dtype),
        compiler_params=_params(2),
    )(w_t, x)


def _mm_nt_blocked_kernel(w_ref, x_ref, o_ref):
    res = lax.dot_general(w_ref[...], x_ref[...], (((1,), (1,)), ((), ())),
                          preferred_element_type=F32).astype(o_ref.dtype)
    slab = o_ref.shape[2]
    for j in range(o_ref.shape[0]):
        o_ref[j] = res[:, j * slab:(j + 1) * slab]


def _matmul_nt_blocked(w_t, x, out_dtype, tm, slab):
    r, k = w_t.shape
    m = x.shape[0]
    return pl.pallas_call(
        _mm_nt_blocked_kernel,
        grid=(m // tm,),
        in_specs=[pl.BlockSpec((r, k), lambda i: (0, 0)),
                  pl.BlockSpec((tm, k), lambda i: (i, 0))],
        out_specs=pl.BlockSpec((tm // slab, r, slab), lambda i: (i, 0, 0)),
        out_shape=jax.ShapeDtypeStruct((m // slab, r, slab), out_dtype),
        compiler_params=_params(1),
    )(w_t, x)


def _pool_kernel(v_ref, halo_ref, w_ref, sc_ref, o_ref, buf_ref, *, seq_tiles, halo):
    ts = v_ref.shape[0]
    group = w_ref.shape[1]
    tile = pl.program_id(0) % seq_tiles
    buf_ref[0:halo, :] = jnp.where(tile == 0, 0.0, halo_ref[...])
    buf_ref[halo:, :] = v_ref[...]
    pos = tile * ts + lax.broadcasted_iota(I32, (ts, 1), 0)
    for g, win in enumerate(POOL_WINDOWS):
        cols = slice(g * group, (g + 1) * group)
        tok = buf_ref[halo:halo + ts, cols]
        acc = tok
        for j in range(1, win):
            acc = acc + buf_ref[halo - j:halo - j + ts, cols]
        cnt = jnp.minimum(pos + 1, win).astype(F32)
        pooled = acc / cnt - tok
        mixed = jnp.dot(pooled.astype(BF16), w_ref[g], preferred_element_type=F32)
        o_ref[:, cols] = (mixed * sc_ref[:, cols]).astype(o_ref.dtype)


def _pool_mixer(p_pool, pool_w, pool_scale, seq, ts):
    n, d_pool = p_pool.shape
    halo = max(POOL_WINDOWS)
    seq_tiles = seq // ts
    return pl.pallas_call(
        functools.partial(_pool_kernel, seq_tiles=seq_tiles, halo=halo),
        grid=(n // ts,),
        in_specs=[pl.BlockSpec((ts, d_pool), lambda r: (r, 0)),
                  pl.BlockSpec((halo, d_pool), lambda r: (jnp.maximum(r * (ts // halo) - 1, 0), 0)),
                  pl.BlockSpec(pool_w.shape, lambda r: (0, 0, 0)),
                  pl.BlockSpec((1, d_pool), lambda r: (0, 0))],
        out_specs=pl.BlockSpec((ts, d_pool), lambda r: (r, 0)),
        out_shape=jax.ShapeDtypeStruct((n, d_pool), BF16),
        scratch_shapes=[pltpu.VMEM((halo + ts, d_pool), F32)],
        compiler_params=_params(1),
    )(p_pool, p_pool, pool_w, pool_scale)


def _bias_kernel(rb_ref, o_ref, *, tq):
    n_heads, n_slabs, ks, _ = o_ref.shape
    max_exact = N_BUCKETS // 2
    for j in range(n_slabs):
        dist = (lax.broadcasted_iota(I32, (ks, tq), 1) + (tq - ks * j)
                - lax.broadcasted_iota(I32, (ks, tq), 0))
        dist = jnp.maximum(dist, 0)
        d = jnp.maximum(dist, 1).astype(F32)
        large = max_exact + (jnp.log(d / max_exact) / math.log(MAX_DISTANCE / max_exact)
                             * (N_BUCKETS - max_exact)).astype(I32)
        large = jnp.minimum(large, N_BUCKETS - 1)
        bucket = jnp.where(dist < max_exact, dist, large)
        for h in range(n_heads):
            def pick(b, acc, h=h, bucket=bucket):
                return jnp.where(bucket == b, rb_ref[b, h], acc)
            bias = lax.fori_loop(0, N_BUCKETS, pick, jnp.zeros((ks, tq), F32))
            o_ref[h, j] = bias - rb_ref[N_BUCKETS - 1, h]


def _bias_tiles(rel_bias, tq):
    n_heads = rel_bias.shape[1]
    return pl.pallas_call(
        functools.partial(_bias_kernel, tq=tq),
        in_specs=[pl.BlockSpec(memory_space=pltpu.SMEM)],
        out_specs=pl.BlockSpec(memory_space=pltpu.VMEM),
        out_shape=jax.ShapeDtypeStruct((n_heads, 2, tq, tq), F32),
        compiler_params=pltpu.CompilerParams(vmem_limit_bytes=VMEM_LIMIT),
    )(rel_bias)


def _attn_kernel(qT_ref, qiT_ref, wiT_ref, k_ref, ki2_ref, vT_ref, bias_ref, o_ref,
                 key_ref, m_ref, l_ref, acc_ref, s_ref, *, top_k, tq, hd):
    ks = LANES
    halves = tq // ks
    qb = pl.program_id(1)
    n_blk = qb + 1
    q_pos = qb * tq + lax.broadcasted_iota(I32, (1, tq), 1)

    def key_pos(r0):
        return r0 + lax.broadcasted_iota(I32, (ks, 1), 0)

    w_all = wiT_ref[...]

    def score_blk(kb, carry):
        for hf in range(halves):
            r0 = pl.multiple_of(kb * tq + hf * ks, ks)
            acc = jnp.zeros((ks, tq), F32)
            for p in range(IDX_HEADS // 2):
                qp = qiT_ref[p * LANES:(p + 1) * LANES, :]
                z0 = jnp.dot(ki2_ref[pl.ds(r0, ks), 0:LANES], qp, preferred_element_type=F32)
                z1 = jnp.dot(ki2_ref[pl.ds(r0, ks), LANES:2 * LANES], qp,
                             preferred_element_type=F32)
                acc = acc + w_all[2 * p:2 * p + 1, :] * jnp.maximum(z0, 0.0)
                acc = acc + w_all[2 * p + 1:2 * p + 2, :] * jnp.maximum(z1, 0.0)
            acc = jnp.where(key_pos(r0) <= q_pos, acc, NEG_INF)
            bits = pltpu.bitcast(acc, I32)
            key_ref[pl.ds(r0, ks), :] = bits ^ ((bits >> 31) & 0x7FFFFFFF)
        return carry

    lax.fori_loop(0, n_blk, score_blk, 0)

    def bit_pending(state):
        i, _, cnt_p = state
        return jnp.logical_and(i < 32, jnp.max(cnt_p) > top_k)

    def bit_step(state):
        i, prefix, cnt_p = state
        cand = prefix | lax.shift_left(jnp.int32(1), 31 - i)
        cand_s = cand ^ INT_MIN

        def count_blk(kb, c8):
            for hf in range(halves):
                r0 = pl.multiple_of(kb * tq + hf * ks, ks)
                hit = (key_ref[pl.ds(r0, ks), :] >= cand_s).astype(F32)
                c8 = c8 + jnp.sum(hit.reshape(ks // SUBLANES, SUBLANES, tq), axis=0)
            return c8

        c8 = lax.fori_loop(0, n_blk, count_blk, jnp.zeros((SUBLANES, tq), F32))
        cnt = jnp.sum(c8, axis=0, keepdims=True)
        take = cnt >= top_k
        return i + 1, jnp.where(take, cand, prefix), jnp.where(take, cnt, cnt_p)

    n_rows = (n_blk * tq).astype(F32)
    state = (jnp.int32(0), jnp.zeros((1, tq), I32), jnp.full((1, tq), n_rows, F32))
    state = lax.fori_loop(0, UNCHECKED_BITS, lambda _, s: bit_step(s), state)
    _, prefix, _ = lax.while_loop(bit_pending, bit_step, state)
    thr = prefix ^ INT_MIN

    def mask_blk(kb, carry):
        for hf in range(halves):
            r0 = pl.multiple_of(kb * tq + hf * ks, ks)
            sel = (key_ref[pl.ds(r0, ks), :] >= thr) & (key_pos(r0) <= q_pos)
            key_ref[pl.ds(r0, ks), :] = pltpu.bitcast(jnp.where(sel, 0.0, NEG_INF).astype(F32), I32)
        return carry

    lax.fori_loop(0, n_blk, mask_blk, 0)

    m_ref[...] = jnp.full(m_ref.shape, M_FLOOR, F32)
    l_ref[...] = jnp.zeros(l_ref.shape, F32)
    acc_ref[...] = jnp.zeros(acc_ref.shape, F32)

    def step(kb, carry, with_bias):
        r0 = pl.multiple_of(kb * tq, tq)
        mask = pltpu.bitcast(key_ref[pl.ds(r0, tq), :], F32)
        maxes = []
        for h in range(N_HEADS):
            hs = slice(h * hd, (h + 1) * hd)
            s = jnp.dot(k_ref[pl.ds(r0, tq), hs], qT_ref[hs, :], preferred_element_type=F32) + mask
            if with_bias:
                s = s + bias_ref[h, kb - (qb - 1)]
            s_ref[h] = s
            maxes.append(jnp.max(s, axis=0, keepdims=True))
        for h in range(N_HEADS):
            hs = slice(h * hd, (h + 1) * hd)
            m = m_ref[h]
            m_new = jnp.maximum(m, maxes[h])
            alpha = jnp.exp(m - m_new)
            p = jnp.exp(s_ref[h] - m_new)
            l_ref[h] = alpha * l_ref[h] + jnp.sum(p, axis=0, keepdims=True)
            acc_ref[h] = alpha * acc_ref[h] + jnp.dot(vT_ref[kb, hs, :], p.astype(BF16),
                                                      preferred_element_type=F32)
            m_ref[h] = m_new
        return carry

    near0 = jnp.maximum(qb - 1, 0)
    lax.fori_loop(0, near0, functools.partial(step, with_bias=False), 0)
    lax.fori_loop(near0, n_blk, functools.partial(step, with_bias=True), 0)
    for h in range(N_HEADS):
        o_ref[h * hd:(h + 1) * hd, :] = (acc_ref[h] / l_ref[h]).astype(o_ref.dtype)


def _layer_norm_t(r, g, b):
    mu = jnp.mean(r, axis=0, keepdims=True)
    c = r - mu
    var = jnp.mean(c * c, axis=0, keepdims=True)
    return c * lax.rsqrt(var + LN_EPS) * g + b


def _outproj_kernel(pool_ref, attnT_ref, x_ref, wpT_ref, waT_ref, g_ref, b_ref, hT_ref, hTb_ref,
                    *, alpha):
    mix = lax.dot_general(wpT_ref[...], pool_ref[...], (((1,), (1,)), ((), ())),
                          preferred_element_type=F32)
    mix = mix + jnp.dot(waT_ref[...], attnT_ref[...], preferred_element_type=F32)
    h = _layer_norm_t(alpha * x_ref[...].T + mix, g_ref[...], b_ref[...])
    hT_ref[...] = h
    hTb_ref[...] = h.astype(BF16)


def _out_projection(pool_out, attnT, x2, wpT, waT, g, b, alpha, tm):
    n, d = x2.shape
    d_pool = pool_out.shape[1]
    d_attn = attnT.shape[0]
    return pl.pallas_call(
        functools.partial(_outproj_kernel, alpha=alpha),
        grid=(n // tm,),
        in_specs=[pl.BlockSpec((tm, d_pool), lambda i: (i, 0)),
                  pl.BlockSpec((d_attn, tm), lambda i: (0, i)),
                  pl.BlockSpec((tm, d), lambda i: (i, 0)),
                  _resident((d, d_pool), lambda i: (0, 0)),
                  _resident((d, d_attn), lambda i: (0, 0)),
                  _resident((d, 1), lambda i: (0, 0)),
                  _resident((d, 1), lambda i: (0, 0))],
        out_specs=[pl.BlockSpec((d, tm), lambda i: (0, i)),
                   pl.BlockSpec((d, tm), lambda i: (0, i))],
        out_shape=[jax.ShapeDtypeStruct((d, n), F32), jax.ShapeDtypeStruct((d, n), BF16)],
        compiler_params=_params(1),
    )(pool_out, attnT, x2, wpT, waT, g, b)


def _peer_query_kernel(hTb_ref, wqT_ref, sk_ref, sT_ref):
    qhT = jnp.dot(wqT_ref[...], hTb_ref[...], preferred_element_type=F32).astype(BF16)
    half = sk_ref.shape[2]
    for c in range(sT_ref.shape[0]):
        sT_ref[c] = jnp.dot(sk_ref[c % 2], qhT[c * half:(c + 1) * half, :],
                            preferred_element_type=F32)


def _peer_query(hTb, wqT, sub_keys, tm):
    d, n = hTb.shape
    dq = wqT.shape[0]
    n_keys, half = sub_keys.shape[1:]
    n_slices = dq // half
    return pl.pallas_call(
        _peer_query_kernel,
        grid=(n // tm,),
        in_specs=[pl.BlockSpec((d, tm), lambda i: (0, i)),
                  _resident((dq, d), lambda i: (0, 0)),
                  _resident(sub_keys.shape, lambda i: (0, 0, 0))],
        out_specs=pl.BlockSpec((n_slices, n_keys, tm), lambda i: (0, 0, i)),
        out_shape=jax.ShapeDtypeStruct((n_slices, n_keys, n), F32),
        compiler_params=_params(1),
    )(hTb, wqT, sub_keys)


def _bitonic_sort_desc(xs):
    xs = list(xs)
    n = len(xs)
    k = 2
    while k <= n:
        j = k // 2
        while j >= 1:
            for i in range(n):
                o = i ^ j
                if o > i:
                    hi, lo = jnp.maximum(xs[i], xs[o]), jnp.minimum(xs[i], xs[o])
                    xs[i], xs[o] = (hi, lo) if (i & k) == 0 else (lo, hi)
            j //= 2
        k *= 2
    return xs


def _bitonic_merge_desc(xs):
    xs = list(xs)
    n = len(xs)
    j = n // 2
    while j >= 1:
        for i in range(n):
            o = i ^ j
            if o > i:
                xs[i], xs[o] = jnp.maximum(xs[i], xs[o]), jnp.minimum(xs[i], xs[o])
        j //= 2
    return xs


def _top_of_two_sorted(a, b):
    n = len(a)
    return [jnp.maximum(a[i], b[n - 1 - i]) for i in range(n)]


def _top16_over_keys(s):
    groups = s.shape[0] // SUBLANES
    xs = _bitonic_sort_desc([s[g * SUBLANES:(g + 1) * SUBLANES, :] for g in range(groups)])
    shift = SUBLANES // 2
    while shift >= 1:
        other = [pltpu.roll(a, shift, 0) for a in xs]
        xs = _bitonic_merge_desc(_top_of_two_sorted(xs, other))
        shift //= 2
    return xs


def _peer_route_kernel(sT_ref, s1_ref, s2_ref, thr_ref):
    k = PEER_TOPK
    assert PEER_HEADS == SUBLANES, "one head per sublane in the candidate stage"
    sub = lax.broadcasted_iota(I32, (SUBLANES, sT_ref.shape[2]), 0)
    a, b = None, None
    for h in range(PEER_HEADS):
        ah = _top16_over_keys(sT_ref[2 * h])
        bh = _top16_over_keys(sT_ref[2 * h + 1])
        a = ah if a is None else [jnp.where(sub == h, x, y) for x, y in zip(ah, a)]
        b = bh if b is None else [jnp.where(sub == h, x, y) for x, y in zip(bh, b)]
    pairs = [(i, j) for i in range(k) for j in range(k) if (i + 1) * (j + 1) <= k]
    cands = [a[i] + b[j] for i, j in pairs]
    pad = jnp.full_like(cands[0], -jnp.inf)
    padded = cands + [pad] * (-len(cands) % k)
    top = _bitonic_sort_desc(padded[:k])
    for g in range(1, len(padded) // k):
        nxt = _bitonic_sort_desc(padded[g * k:(g + 1) * k])
        top = _top_of_two_sorted(top, nxt)
        if g + 1 < len(padded) // k:
            top = _bitonic_merge_desc(top)
    cmax = a[0] + b[0]
    thr = functools.reduce(jnp.minimum, top)
    z = functools.reduce(jnp.add, [jnp.exp(t - cmax) for t in top])
    cc = -cmax - jnp.log(z)
    a2 = [x * LOG2E for x in a]
    b2 = [(x + cc) * LOG2E for x in b]
    thr_ref[...] = functools.reduce(jnp.minimum, [jnp.where(c >= thr, a2[i] + b2[j], jnp.inf)
                                                  for c, (i, j) in zip(cands, pairs)])
    for h in range(PEER_HEADS):
        s1_ref[h] = sT_ref[2 * h] * LOG2E
        s2_ref[h] = (sT_ref[2 * h + 1] + cc[h:h + 1, :]) * LOG2E


def _peer_route(sT, tt):
    n_slices, n_keys, n = sT.shape
    big = pl.BlockSpec((PEER_HEADS, n_keys, tt), lambda i: (0, 0, i))
    return pl.pallas_call(
        _peer_route_kernel,
        grid=(n // tt,),
        in_specs=[pl.BlockSpec((n_slices, n_keys, tt), lambda i: (0, 0, i))],
        out_specs=[big, big, pl.BlockSpec((PEER_HEADS, tt), lambda i: (0, i))],
        out_shape=[jax.ShapeDtypeStruct((PEER_HEADS, n_keys, n), F32)] * 2
        + [jax.ShapeDtypeStruct((PEER_HEADS, n), F32)],
        compiler_params=_params(1),
    )(sT)


def _peer_dense_kernel(hTb_ref, u_ref, vT_ref, s1_ref, s2_ref, thr_ref, o_ref, *, chunk):
    ei = pl.program_id(1)
    te = u_ref.shape[0]
    rows = te // PEER_KEYS
    rows_per_chunk = chunk // PEER_KEYS

    @pl.when(ei == 0)
    def _():
        o_ref[...] = jnp.zeros_like(o_ref)

    i0 = pl.multiple_of(ei * rows, rows)

    def pre_act(c):
        return jnp.dot(u_ref[c * chunk:(c + 1) * chunk, :], hTb_ref[...],
                       preferred_element_type=F32)

    n_chunks = te // chunk
    act_next = pre_act(0)
    for c in range(n_chunks):
        act = act_next
        if c + 1 < n_chunks:
            act_next = pre_act(c + 1)
        gates = []
        for rr in range(rows_per_chunk):
            r = c * rows_per_chunk + rr
            gate = jnp.zeros((PEER_KEYS, act.shape[1]), F32)
            for h in range(PEER_HEADS):
                t = s1_ref[h, pl.ds(i0, rows), :][r:r + 1, :] + s2_ref[h]
                gate = gate + jnp.where(t >= thr_ref[h:h + 1, :], jnp.exp2(t), 0.0)
            gates.append(gate)
        act = 0.5 * act * (1.0 + lax.erf(act * math.sqrt(0.5)))
        p = jnp.concatenate(
            [(gates[rr] * act[rr * PEER_KEYS:(rr + 1) * PEER_KEYS, :]).astype(BF16)
             for rr in range(rows_per_chunk)], axis=0)
        o_ref[...] += jnp.dot(vT_ref[:, c * chunk:(c + 1) * chunk], p, preferred_element_type=F32)


def _peer_dense(hTb, u, vT, s1, s2, thr, tt, te, chunk):
    d, n = hTb.shape
    n_exp = u.shape[0]
    n_keys = s1.shape[1]
    big = pl.BlockSpec((PEER_HEADS, n_keys, tt), lambda t, e: (0, 0, t))
    return pl.pallas_call(
        functools.partial(_peer_dense_kernel, chunk=chunk),
        grid=(n // tt, n_exp // te),
        in_specs=[pl.BlockSpec((d, tt), lambda t, e: (0, t)),
                  pl.BlockSpec((te, d), lambda t, e: (e, 0)),
                  pl.BlockSpec((d, te), lambda t, e: (0, e)),
                  big, big,
                  pl.BlockSpec((PEER_HEADS, tt), lambda t, e: (0, t))],
        out_specs=pl.BlockSpec((d, tt), lambda t, e: (0, t)),
        out_shape=jax.ShapeDtypeStruct((d, n), F32),
        compiler_params=_params(2),
    )(hTb, u, vT, s1, s2, thr)


def _final_ln_kernel(hT_ref, fT_ref, g_ref, b_ref, o_ref, *, alpha):
    y = _layer_norm_t(alpha * hT_ref[...] + fT_ref[...], g_ref[...], b_ref[...])
    o_ref[...] = y.T


def _final_ln(hT, ffnT, g, b, alpha, tm):
    d, n = hT.shape
    return pl.pallas_call(
        functools.partial(_final_ln_kernel, alpha=alpha),
        grid=(n // tm,),
        in_specs=[pl.BlockSpec((d, tm), lambda i: (0, i)),
                  pl.BlockSpec((d, tm), lambda i: (0, i)),
                  _resident((d, 1), lambda i: (0, 0)),
                  _resident((d, 1), lambda i: (0, 0))],
        out_specs=pl.BlockSpec((tm, d), lambda i: (i, 0)),
        out_shape=jax.ShapeDtypeStruct((n, d), F32),
        compiler_params=_params(1),
    )(hT, ffnT, g, b)


def _tile(n, pref):
    t = min(pref, n)
    while n % t:
        t -= LANES
    return t


def _layer(h2, batch, seq, w_in, pool_w, pool_scale, bias, w_out, ln1_g, ln1_b,
           peer_wq, sub_keys, peer_u, peer_v, ln2_g, ln2_b, alpha, tq):
    n, d = h2.shape
    d_pool = pool_w.shape[0] * pool_w.shape[1]
    d_attn = w_out.shape[0] - d_pool
    hd = d_attn // N_HEADS
    d_qi = IDX_HEADS * IDX_DIM
    top_k = min(MAX_TOPK, seq // 4)

    o = 0
    w_pool = w_in[:, o:o + d_pool]; o += d_pool
    w_q = w_in[:, o:o + d_attn] * hd ** -0.5; o += d_attn
    w_k = w_in[:, o:o + d_attn]; o += d_attn
    w_v = w_in[:, o:o + d_attn]; o += d_attn
    w_qi = w_in[:, o:o + d_qi]; o += d_qi
    w_ki = w_in[:, o:o + IDX_DIM]; o += IDX_DIM
    w_wi = w_in[:, o:o + IDX_HEADS] * (IDX_HEADS * IDX_DIM) ** -0.5
    zeros = jnp.zeros_like(w_ki)
    w_nat = jnp.concatenate([w_k, w_ki, zeros, zeros, w_ki], axis=1).astype(BF16)
    w_qT = jnp.concatenate([w_q.T, w_qi.T], axis=0).astype(BF16)

    x_bf = h2.astype(BF16)
    tm = _tile(n, 1024)
    p_pool = _matmul(x_bf, w_pool.astype(BF16), F32, tm, _tile(d_pool, 512))
    k_nat = _matmul(x_bf, w_nat, BF16, tm, w_nat.shape[1])
    qT_all = _matmul_nt(w_qT, x_bf, BF16, _tile(w_qT.shape[0], 1024), tm)
    vT_blk = _matmul_nt_blocked(w_v.T.astype(BF16), x_bf, BF16, _tile(seq, 512), tq)
    wiT = _matmul_nt(w_wi.T.astype(BF16), x_bf, F32, IDX_HEADS, tm)

    pool_out = _pool_mixer(p_pool, pool_w.astype(BF16), pool_scale.reshape(1, d_pool), seq,
                           _tile(seq, 256))

    attnT = _sparse_attention_split(qT_all, wiT, k_nat, vT_blk, bias, batch, seq, tq, top_k, d_attn)

    w_pT = w_out[:d_pool].T.astype(BF16)
    w_aT = w_out[d_pool:].T.astype(BF16)
    hT, hTb = _out_projection(pool_out, attnT, h2, w_pT, w_aT, ln1_g.reshape(d, 1),
                              ln1_b.reshape(d, 1), alpha, _tile(n, 256))

    sT = _peer_query(hTb, peer_wq.T.astype(BF16), sub_keys.astype(BF16), _tile(n, 256))
    s1, s2, thr = _peer_route(sT, _tile(n, 256))
    ffnT = _peer_dense(hTb, peer_u.astype(BF16), peer_v.T.astype(BF16), s1, s2, thr,
                       _tile(n, 512), 8 * PEER_KEYS, 4 * PEER_KEYS)
    return _final_ln(hT, ffnT, ln2_g.reshape(d, 1), ln2_b.reshape(d, 1), alpha, _tile(n, 256))


def _sparse_attention_split(qT_all, wiT, k_nat, vT_blk, bias, batch, seq, tq, top_k, d_attn):
    n = batch * seq
    nq = seq // tq
    hd = d_attn // N_HEADS
    d_qi = IDX_HEADS * IDX_DIM
    assert d_qi == d_attn, "q and indexer-q row blocks must have equal height"
    tok0 = lambda b, q: (0, b * nq + q)
    tok1 = lambda b, q: (1, b * nq + q)
    return pl.pallas_call(
        functools.partial(_attn_kernel, top_k=top_k, tq=tq, hd=hd),
        grid=(batch, nq),
        in_specs=[pl.BlockSpec((d_attn, tq), tok0),
                  pl.BlockSpec((d_qi, tq), tok1),
                  pl.BlockSpec((IDX_HEADS, tq), tok0),
                  _resident((seq, d_attn), lambda b, q: (b, 0)),
                  _resident((seq, 4 * IDX_DIM), lambda b, q: (b, d_attn // (4 * IDX_DIM))),
                  _resident((nq, d_attn, tq), lambda b, q: (b, 0, 0)),
                  _resident(bias.shape, lambda b, q: (0, 0, 0, 0))],
        out_specs=pl.BlockSpec((d_attn, tq), tok0),
        out_shape=jax.ShapeDtypeStruct((d_attn, n), BF16),
        scratch_shapes=[pltpu.VMEM((seq, tq), I32),
                        pltpu.VMEM((N_HEADS, 1, tq), F32),
                        pltpu.VMEM((N_HEADS, 1, tq), F32),
                        pltpu.VMEM((N_HEADS, hd, tq), F32),
                        pltpu.VMEM((N_HEADS, tq, tq), F32)],
        compiler_params=_params(2),
    )(qT_all, qT_all, wiT, k_nat, k_nat, vT_blk, bias)


def kernel(x, w_in, pool_w, pool_scale, rel_bias, w_out, ln1_g, ln1_b, peer_wq, peer_subkeys,
           peer_u, peer_v, ln2_g, ln2_b):
    batch, seq, d = x.shape
    depth = w_in.shape[0]
    alpha = (2 * depth) ** 0.25
    tq = min(256, seq)
    bias = _bias_tiles(rel_bias, tq)
    h2 = x.reshape(batch * seq, d)
    for l in range(depth):
        h2 = _layer(h2, batch, seq, w_in[l], pool_w[l], pool_scale[l], bias, w_out[l],
                    ln1_g[l], ln1_b[l], peer_wq[l], peer_subkeys[l], peer_u[l], peer_v[l],
                    ln2_g[l], ln2_b[l], alpha, tq)
    return h2.reshape(batch, seq, d)
```

```python
import functools
import math

import jax
import jax.numpy as jnp
from jax import lax
from jax.experimental import pallas as pl
from jax.experimental.pallas import tpu as pltpu

F32 = jnp.float32
BF16 = jnp.bfloat16
I32 = jnp.int32

LANES = 128
SUBLANES = 8
VMEM_LIMIT = 56 * 1024 * 1024

POOL_WINDOWS = (2, 4, 8, 16)
N_HEADS = 8
IDX_HEADS = 16
IDX_DIM = 64
MAX_TOPK = 256
N_BUCKETS = 32
MAX_DISTANCE = 128
PEER_HEADS = 8
PEER_KEYS = 128
PEER_TOPK = 16
LN_EPS = 1e-5
NEG_INF = -1e30
M_FLOOR = -1e20
BISECT_UNCHECKED = 14
BISECT_CAP = 48
LOG2E = 1.0 / math.log(2.0)


def _params(n_grid):
    return pltpu.CompilerParams(dimension_semantics=("arbitrary",) * n_grid,
                                vmem_limit_bytes=VMEM_LIMIT)


def _resident(shape, index_map):
    return pl.BlockSpec(shape, index_map, pipeline_mode=pl.Buffered(1))


def _mm_kernel(a_ref, b_ref, o_ref):
    o_ref[...] = jnp.dot(a_ref[...], b_ref[...], preferred_element_type=F32).astype(o_ref.dtype)


def _matmul(a, b, out_dtype, tm, tn):
    m, k = a.shape
    n = b.shape[1]
    return pl.pallas_call(
        _mm_kernel,
        grid=(n // tn, m // tm),
        in_specs=[pl.BlockSpec((tm, k), lambda j, i: (i, 0)),
                  pl.BlockSpec((k, tn), lambda j, i: (0, j))],
        out_specs=pl.BlockSpec((tm, tn), lambda j, i: (i, j)),
        out_shape=jax.ShapeDtypeStruct((m, n), out_dtype),
        compiler_params=_params(2),
    )(a, b)


def _mm_nt_kernel(w_ref, x_ref, o_ref):
    o_ref[...] = lax.dot_general(w_ref[...], x_ref[...], (((1,), (1,)), ((), ())),
                                 preferred_element_type=F32).astype(o_ref.dtype)


def _matmul_nt(w_t, x, out_dtype, tr, tm):
    r, k = w_t.shape
    m = x.shape[0]
    return pl.pallas_call(
        _mm_nt_kernel,
        grid=(r // tr, m // tm),
        in_specs=[pl.BlockSpec((tr, k), lambda j, i: (j, 0)),
                  pl.BlockSpec((tm, k), lambda j, i: (i, 0))],
        out_specs=pl.BlockSpec((tr, tm), lambda j, i: (j, i)),
        out_shape=jax.ShapeDtypeStruct((r, m), out_dtype),
        compiler_params=_params(2),
    )(w_t, x)


def _mm_nt_blocked_kernel(w_ref, x_ref, o_ref):
    res = lax.dot_general(w_ref[...], x_ref[...], (((1,), (1,)), ((), ())),
                          preferred_element_type=F32).astype(o_ref.dtype)
    slab = o_ref.shape[2]
    for j in range(o_ref.shape[0]):
        o_ref[j] = res[:, j * slab:(j + 1) * slab]


def _matmul_nt_blocked(w_t, x, out_dtype, tm, slab):
    r, k = w_t.shape
    m = x.shape[0]
    return pl.pallas_call(
        _mm_nt_blocked_kernel,
        grid=(m // tm,),
        in_specs=[pl.BlockSpec((r, k), lambda i: (0, 0)),
                  pl.BlockSpec((tm, k), lambda i: (i, 0))],
        out_specs=pl.BlockSpec((tm // slab, r, slab), lambda i: (i, 0, 0)),
        out_shape=jax.ShapeDtypeStruct((m // slab, r, slab), out_dtype),
        compiler_params=_params(1),
    )(w_t, x)


def _pool_kernel(v_ref, halo_ref, w_ref, sc_ref, o_ref, buf_ref, *, seq_tiles, halo):
    ts = v_ref.shape[0]
    group = w_ref.shape[1]
    tile = pl.program_id(0) % seq_tiles
    buf_ref[0:halo, :] = jnp.where(tile == 0, 0.0, halo_ref[...])
    buf_ref[halo:, :] = v_ref[...]
    pos = tile * ts + lax.broadcasted_iota(I32, (ts, 1), 0)
    for g, win in enumerate(POOL_WINDOWS):
        cols = slice(g * group, (g + 1) * group)
        tok = buf_ref[halo:halo + ts, cols]
        acc = tok
        for j in range(1, win):
            acc = acc + buf_ref[halo - j:halo - j + ts, cols]
        cnt = jnp.minimum(pos + 1, win).astype(F32)
        pooled = acc / cnt - tok
        mixed = jnp.dot(pooled.astype(BF16), w_ref[g], preferred_element_type=F32)
        o_ref[:, cols] = (mixed * sc_ref[:, cols]).astype(o_ref.dtype)


def _pool_mixer(p_pool, pool_w, pool_scale, seq, ts):
    n, d_pool = p_pool.shape
    halo = max(POOL_WINDOWS)
    seq_tiles = seq // ts
    return pl.pallas_call(
        functools.partial(_pool_kernel, seq_tiles=seq_tiles, halo=halo),
        grid=(n // ts,),
        in_specs=[pl.BlockSpec((ts, d_pool), lambda r: (r, 0)),
                  pl.BlockSpec((halo, d_pool), lambda r: (jnp.maximum(r * (ts // halo) - 1, 0), 0)),
                  pl.BlockSpec(pool_w.shape, lambda r: (0, 0, 0)),
                  pl.BlockSpec((1, d_pool), lambda r: (0, 0))],
        out_specs=pl.BlockSpec((ts, d_pool), lambda r: (r, 0)),
        out_shape=jax.ShapeDtypeStruct((n, d_pool), BF16),
        scratch_shapes=[pltpu.VMEM((halo + ts, d_pool), F32)],
        compiler_params=_params(1),
    )(p_pool, p_pool, pool_w, pool_scale)


def _bias_kernel(rb_ref, o_ref, *, tq):
    n_heads, n_slabs, ks, _ = o_ref.shape
    max_exact = N_BUCKETS // 2
    for j in range(n_slabs):
        dist = (lax.broadcasted_iota(I32, (ks, tq), 1) + (tq - ks * j)
                - lax.broadcasted_iota(I32, (ks, tq), 0))
        dist = jnp.maximum(dist, 0)
        d = jnp.maximum(dist, 1).astype(F32)
        large = max_exact + (jnp.log(d / max_exact) / math.log(MAX_DISTANCE / max_exact)
                             * (N_BUCKETS - max_exact))
        coord = jnp.where(dist < max_exact, dist.astype(F32), large)
        for h in range(n_heads):
            def pick(b, acc, h=h, coord=coord):
                return jnp.where(coord >= b.astype(F32), rb_ref[b, h], acc)
            bias = lax.fori_loop(1, N_BUCKETS, pick, jnp.full((ks, tq), rb_ref[0, h], F32))
            o_ref[h, j] = bias - rb_ref[N_BUCKETS - 1, h]


def _bias_tiles(rel_bias, tq):
    n_heads = rel_bias.shape[1]
    return pl.pallas_call(
        functools.partial(_bias_kernel, tq=tq),
        in_specs=[pl.BlockSpec(memory_space=pltpu.SMEM)],
        out_specs=pl.BlockSpec(memory_space=pltpu.VMEM),
        out_shape=jax.ShapeDtypeStruct((n_heads, 2, tq, tq), F32),
        compiler_params=pltpu.CompilerParams(vmem_limit_bytes=VMEM_LIMIT),
    )(rel_bias)


def _attn_kernel(qT_ref, qiT_ref, wiT_ref, k_ref, ki2_ref, vT_ref, bias_ref, o_ref,
                 key_ref, m_ref, l_ref, acc_ref, s_ref, *, top_k, tq, hd):
    ks = LANES
    halves = tq // ks
    qb = pl.program_id(1)
    n_blk = qb + 1
    q_pos = qb * tq + lax.broadcasted_iota(I32, (1, tq), 1)

    def key_pos(r0):
        return r0 + lax.broadcasted_iota(I32, (ks, 1), 0)

    w_all = wiT_ref[...]

    def score_blk(kb, carry):
        lo, hi = carry
        for hf in range(halves):
            r0 = pl.multiple_of(kb * tq + hf * ks, ks)
            acc = jnp.zeros((ks, tq), F32)
            for p in range(IDX_HEADS // 2):
                qp = qiT_ref[p * LANES:(p + 1) * LANES, :]
                z0 = jnp.dot(ki2_ref[pl.ds(r0, ks), 0:LANES], qp, preferred_element_type=F32)
                z1 = jnp.dot(ki2_ref[pl.ds(r0, ks), LANES:2 * LANES], qp,
                             preferred_element_type=F32)
                acc = acc + w_all[2 * p:2 * p + 1, :] * jnp.maximum(z0, 0.0)
                acc = acc + w_all[2 * p + 1:2 * p + 2, :] * jnp.maximum(z1, 0.0)
            causal = key_pos(r0) <= q_pos
            key_ref[pl.ds(r0, ks), :] = jnp.where(causal, acc, NEG_INF)
            hi = jnp.maximum(hi, jnp.max(jnp.where(causal, acc, -jnp.inf), axis=0, keepdims=True))
            lo = jnp.minimum(lo, jnp.min(jnp.where(causal, acc, jnp.inf), axis=0, keepdims=True))
        return lo, hi

    lo, hi = lax.fori_loop(0, n_blk, score_blk,
                           (jnp.full((1, tq), jnp.inf, F32), jnp.full((1, tq), -jnp.inf, F32)))

    def pending(state):
        i, _, _, cnt_lo = state
        return jnp.logical_and(i < BISECT_CAP, jnp.max(cnt_lo) > top_k)

    def bisect(state):
        i, lo, hi, cnt_lo = state
        mid = lo + 0.5 * (hi - lo)

        def count_blk(kb, c8):
            for hf in range(halves):
                r0 = pl.multiple_of(kb * tq + hf * ks, ks)
                hit = (key_ref[pl.ds(r0, ks), :] >= mid).astype(F32)
                c8 = c8 + jnp.sum(hit.reshape(ks // SUBLANES, SUBLANES, tq), axis=0)
            return c8

        c8 = lax.fori_loop(0, n_blk, count_blk, jnp.zeros((SUBLANES, tq), F32))
        cnt = jnp.sum(c8, axis=0, keepdims=True)
        take = cnt >= top_k
        return (i + 1, jnp.where(take, mid, lo), jnp.where(take, hi, mid),
                jnp.where(take, cnt, cnt_lo))

    state = (jnp.int32(0), lo, hi, (q_pos + 1).astype(F32))
    state = lax.fori_loop(0, BISECT_UNCHECKED, lambda _, s: bisect(s), state)
    _, thr, _, _ = lax.while_loop(pending, bisect, state)

    def mask_blk(kb, carry):
        for hf in range(halves):
            r0 = pl.multiple_of(kb * tq + hf * ks, ks)
            sel = (key_ref[pl.ds(r0, ks), :] >= thr) & (key_pos(r0) <= q_pos)
            key_ref[pl.ds(r0, ks), :] = jnp.where(sel, 0.0, NEG_INF).astype(F32)
        return carry

    lax.fori_loop(0, n_blk, mask_blk, 0)

    m_ref[...] = jnp.full(m_ref.shape, M_FLOOR, F32)
    l_ref[...] = jnp.zeros(l_ref.shape, F32)
    acc_ref[...] = jnp.zeros(acc_ref.shape, F32)

    def step(kb, carry, with_bias):
        r0 = pl.multiple_of(kb * tq, tq)
        mask = key_ref[pl.ds(r0, tq), :]
        maxes = []
        for h in range(N_HEADS):
            hs = slice(h * hd, (h + 1) * hd)
            s = jnp.dot(k_ref[pl.ds(r0, tq), hs], qT_ref[hs, :], preferred_element_type=F32) + mask
            if with_bias:
                s = s + bias_ref[h, kb - (qb - 1)]
            s_ref[h] = s
            maxes.append(jnp.max(s, axis=0, keepdims=True))
        for h in range(N_HEADS):
            hs = slice(h * hd, (h + 1) * hd)
            m = m_ref[h]
            m_new = jnp.maximum(m, maxes[h])
            alpha = jnp.exp(m - m_new)
            p = jnp.exp(s_ref[h] - m_new)
            l_ref[h] = alpha * l_ref[h] + jnp.sum(p, axis=0, keepdims=True)
            acc_ref[h] = alpha * acc_ref[h] + jnp.dot(vT_ref[kb, hs, :], p.astype(BF16),
                                                      preferred_element_type=F32)
            m_ref[h] = m_new
        return carry

    near0 = jnp.maximum(qb - 1, 0)
    lax.fori_loop(0, near0, functools.partial(step, with_bias=False), 0)
    lax.fori_loop(near0, n_blk, functools.partial(step, with_bias=True), 0)
    for h in range(N_HEADS):
        o_ref[h * hd:(h + 1) * hd, :] = (acc_ref[h] / l_ref[h]).astype(o_ref.dtype)


def _layer_norm_t(r, g, b):
    mu = jnp.mean(r, axis=0, keepdims=True)
    c = r - mu
    var = jnp.mean(c * c, axis=0, keepdims=True)
    return c * lax.rsqrt(var + LN_EPS) * g + b


def _outproj_kernel(pool_ref, attnT_ref, x_ref, wpT_ref, waT_ref, g_ref, b_ref, hT_ref, hTb_ref,
                    *, alpha):
    mix = lax.dot_general(wpT_ref[...], pool_ref[...], (((1,), (1,)), ((), ())),
                          preferred_element_type=F32)
    mix = mix + jnp.dot(waT_ref[...], attnT_ref[...], preferred_element_type=F32)
    h = _layer_norm_t(alpha * x_ref[...].T + mix, g_ref[...], b_ref[...])
    hT_ref[...] = h
    hTb_ref[...] = h.astype(BF16)


def _out_projection(pool_out, attnT, x2, wpT, waT, g, b, alpha, tm):
    n, d = x2.shape
    d_pool = pool_out.shape[1]
    d_attn = attnT.shape[0]
    return pl.pallas_call(
        functools.partial(_outproj_kernel, alpha=alpha),
        grid=(n // tm,),
        in_specs=[pl.BlockSpec((tm, d_pool), lambda i: (i, 0)),
                  pl.BlockSpec((d_attn, tm), lambda i: (0, i)),
                  pl.BlockSpec((tm, d), lambda i: (i, 0)),
                  _resident((d, d_pool), lambda i: (0, 0)),
                  _resident((d, d_attn), lambda i: (0, 0)),
                  _resident((d, 1), lambda i: (0, 0)),
                  _resident((d, 1), lambda i: (0, 0))],
        out_specs=[pl.BlockSpec((d, tm), lambda i: (0, i)),
                   pl.BlockSpec((d, tm), lambda i: (0, i))],
        out_shape=[jax.ShapeDtypeStruct((d, n), F32), jax.ShapeDtypeStruct((d, n), BF16)],
        compiler_params=_params(1),
    )(pool_out, attnT, x2, wpT, waT, g, b)


def _peer_query_kernel(hTb_ref, wqT_ref, sk_ref, sT_ref):
    qhT = jnp.dot(wqT_ref[...], hTb_ref[...], preferred_element_type=F32).astype(BF16)
    half = sk_ref.shape[2]
    for c in range(sT_ref.shape[0]):
        sT_ref[c] = jnp.dot(sk_ref[c % 2], qhT[c * half:(c + 1) * half, :],
                            preferred_element_type=F32)


def _peer_query(hTb, wqT, sub_keys, tm):
    d, n = hTb.shape
    dq = wqT.shape[0]
    n_keys, half = sub_keys.shape[1:]
    n_slices = dq // half
    return pl.pallas_call(
        _peer_query_kernel,
        grid=(n // tm,),
        in_specs=[pl.BlockSpec((d, tm), lambda i: (0, i)),
                  _resident((dq, d), lambda i: (0, 0)),
                  _resident(sub_keys.shape, lambda i: (0, 0, 0))],
        out_specs=pl.BlockSpec((n_slices, n_keys, tm), lambda i: (0, 0, i)),
        out_shape=jax.ShapeDtypeStruct((n_slices, n_keys, n), F32),
        compiler_params=_params(1),
    )(hTb, wqT, sub_keys)


def _bitonic_sort_desc(xs):
    xs = list(xs)
    n = len(xs)
    k = 2
    while k <= n:
        j = k // 2
        while j >= 1:
            for i in range(n):
                o = i ^ j
                if o > i:
                    hi, lo = jnp.maximum(xs[i], xs[o]), jnp.minimum(xs[i], xs[o])
                    xs[i], xs[o] = (hi, lo) if (i & k) == 0 else (lo, hi)
            j //= 2
        k *= 2
    return xs


def _bitonic_merge_desc(xs):
    xs = list(xs)
    n = len(xs)
    j = n // 2
    while j >= 1:
        for i in range(n):
            o = i ^ j
            if o > i:
                xs[i], xs[o] = jnp.maximum(xs[i], xs[o]), jnp.minimum(xs[i], xs[o])
        j //= 2
    return xs


def _top_of_two_sorted(a, b):
    n = len(a)
    return [jnp.maximum(a[i], b[n - 1 - i]) for i in range(n)]


def _top16_over_keys(s):
    groups = s.shape[0] // SUBLANES
    xs = _bitonic_sort_desc([s[g * SUBLANES:(g + 1) * SUBLANES, :] for g in range(groups)])
    shift = SUBLANES // 2
    while shift >= 1:
        other = [pltpu.roll(a, shift, 0) for a in xs]
        xs = _bitonic_merge_desc(_top_of_two_sorted(xs, other))
        shift //= 2
    return xs


def _peer_route_kernel(sT_ref, c1_ref, e1_ref, r2_ref, e2_ref):
    k = PEER_TOPK
    assert PEER_HEADS == SUBLANES, "one head per sublane in the candidate stage"
    sub = lax.broadcasted_iota(I32, (SUBLANES, sT_ref.shape[2]), 0)
    a, b = None, None
    for h in range(PEER_HEADS):
        ah = _top16_over_keys(sT_ref[2 * h])
        bh = _top16_over_keys(sT_ref[2 * h + 1])
        a = ah if a is None else [jnp.where(sub == h, x, y) for x, y in zip(ah, a)]
        b = bh if b is None else [jnp.where(sub == h, x, y) for x, y in zip(bh, b)]
    pairs = [(i, j) for i in range(k) for j in range(k) if (i + 1) * (j + 1) <= k]
    cands = [a[i] + b[j] for i, j in pairs]
    pad = jnp.full_like(cands[0], -jnp.inf)
    padded = cands + [pad] * (-len(cands) % k)
    top = _bitonic_sort_desc(padded[:k])
    for g in range(1, len(padded) // k):
        nxt = _bitonic_sort_desc(padded[g * k:(g + 1) * k])
        top = _top_of_two_sorted(top, nxt)
        if g + 1 < len(padded) // k:
            top = _bitonic_merge_desc(top)
    cmax = a[0] + b[0]
    thr = functools.reduce(jnp.minimum, top)
    z = functools.reduce(jnp.add, [jnp.exp(t - cmax) for t in top])
    log2z = jnp.log(z) * LOG2E
    cut = []
    for r in range(k):
        hits = [jnp.where(c >= thr, 1.0, 0.0) for c, (i, j) in zip(cands, pairs) if i == r]
        cut.append(functools.reduce(jnp.add, hits) - 1.0)
    for h in range(PEER_HEADS):
        row = slice(h, h + 1)
        s1 = sT_ref[2 * h]
        s2 = sT_ref[2 * h + 1]
        c1 = jnp.full(s1.shape, -1.0, F32)
        r2 = jnp.full(s2.shape, float(k), F32)
        for r in reversed(range(k)):
            c1 = jnp.where(s1 == a[r][row, :], cut[r][row, :], c1)
            r2 = jnp.where(s2 == b[r][row, :], float(r), r2)
        c1_ref[h] = c1
        r2_ref[h] = r2.astype(BF16)
        e1_ref[h] = jnp.exp2((s1 - a[0][row, :]) * LOG2E)
        e2_ref[h] = jnp.exp2((s2 - b[0][row, :]) * LOG2E - log2z[row, :]).astype(BF16)


def _peer_route(sT, tt):
    n_slices, n_keys, n = sT.shape
    big = pl.BlockSpec((PEER_HEADS, n_keys, tt), lambda i: (0, 0, i))
    shape = (PEER_HEADS, n_keys, n)
    return pl.pallas_call(
        _peer_route_kernel,
        grid=(n // tt,),
        in_specs=[pl.BlockSpec((n_slices, n_keys, tt), lambda i: (0, 0, i))],
        out_specs=[big, big, big, big],
        out_shape=[jax.ShapeDtypeStruct(shape, F32), jax.ShapeDtypeStruct(shape, F32),
                   jax.ShapeDtypeStruct(shape, BF16), jax.ShapeDtypeStruct(shape, BF16)],
        compiler_params=_params(1),
    )(sT)


def _peer_dense_kernel(hTb_ref, u_ref, vT_ref, c1_ref, e1_ref, r2_ref, e2_ref, o_ref, *, chunk):
    ei = pl.program_id(1)
    te = u_ref.shape[0]
    rows = te // PEER_KEYS
    rows_per_chunk = chunk // PEER_KEYS

    @pl.when(ei == 0)
    def _():
        o_ref[...] = jnp.zeros_like(o_ref)

    i0 = pl.multiple_of(ei * rows, rows)

    def pre_act(c):
        return jnp.dot(u_ref[c * chunk:(c + 1) * chunk, :], hTb_ref[...],
                       preferred_element_type=F32)

    n_chunks = te // chunk
    act_next = pre_act(0)
    for c in range(n_chunks):
        act = act_next
        if c + 1 < n_chunks:
            act_next = pre_act(c + 1)
        gates = []
        for rr in range(rows_per_chunk):
            r = c * rows_per_chunk + rr
            gate = jnp.zeros((PEER_KEYS, act.shape[1]), BF16)
            for h in range(PEER_HEADS):
                cut = c1_ref[h, pl.ds(i0, rows), :][r:r + 1, :].astype(BF16)
                e1 = e1_ref[h, pl.ds(i0, rows), :][r:r + 1, :].astype(BF16)
                gate = gate + jnp.where(r2_ref[h] <= cut, e2_ref[h] * e1, jnp.zeros_like(gate))
            gates.append(gate)
        act = 0.5 * act * (1.0 + lax.erf(act * math.sqrt(0.5)))
        p = jnp.concatenate(
            [gates[rr] * act[rr * PEER_KEYS:(rr + 1) * PEER_KEYS, :].astype(BF16)
             for rr in range(rows_per_chunk)], axis=0)
        o_ref[...] += jnp.dot(vT_ref[:, c * chunk:(c + 1) * chunk], p, preferred_element_type=F32)


def _peer_dense(hTb, u, vT, c1, e1, r2, e2, tt, te, chunk):
    d, n = hTb.shape
    n_exp = u.shape[0]
    n_keys = c1.shape[1]
    big = pl.BlockSpec((PEER_HEADS, n_keys, tt), lambda t, e: (0, 0, t))
    return pl.pallas_call(
        functools.partial(_peer_dense_kernel, chunk=chunk),
        grid=(n // tt, n_exp // te),
        in_specs=[pl.BlockSpec((d, tt), lambda t, e: (0, t)),
                  pl.BlockSpec((te, d), lambda t, e: (e, 0)),
                  pl.BlockSpec((d, te), lambda t, e: (0, e)),
                  big, big, big, big],
        out_specs=pl.BlockSpec((d, tt), lambda t, e: (0, t)),
        out_shape=jax.ShapeDtypeStruct((d, n), F32),
        compiler_params=_params(2),
    )(hTb, u, vT, c1, e1, r2, e2)


def _final_ln_kernel(hT_ref, fT_ref, g_ref, b_ref, o_ref, *, alpha):
    y = _layer_norm_t(alpha * hT_ref[...] + fT_ref[...], g_ref[...], b_ref[...])
    o_ref[...] = y.T


def _final_ln(hT, ffnT, g, b, alpha, tm):
    d, n = hT.shape
    return pl.pallas_call(
        functools.partial(_final_ln_kernel, alpha=alpha),
        grid=(n // tm,),
        in_specs=[pl.BlockSpec((d, tm), lambda i: (0, i)),
                  pl.BlockSpec((d, tm), lambda i: (0, i)),
                  _resident((d, 1), lambda i: (0, 0)),
                  _resident((d, 1), lambda i: (0, 0))],
        out_specs=pl.BlockSpec((tm, d), lambda i: (i, 0)),
        out_shape=jax.ShapeDtypeStruct((n, d), F32),
        compiler_params=_params(1),
    )(hT, ffnT, g, b)


def _tile(n, pref):
    t = min(pref, n)
    while n % t:
        t -= LANES
    return t


def _layer(h2, batch, seq, w_in, pool_w, pool_scale, bias, w_out, ln1_g, ln1_b,
           peer_wq, sub_keys, peer_u, peer_v, ln2_g, ln2_b, alpha, tq):
    n, d = h2.shape
    d_pool = pool_w.shape[0] * pool_w.shape[1]
    d_attn = w_out.shape[0] - d_pool
    hd = d_attn // N_HEADS
    d_qi = IDX_HEADS * IDX_DIM
    top_k = min(MAX_TOPK, seq // 4)

    o = 0
    w_pool = w_in[:, o:o + d_pool]; o += d_pool
    w_q = w_in[:, o:o + d_attn] * hd ** -0.5; o += d_attn
    w_k = w_in[:, o:o + d_attn]; o += d_attn
    w_v = w_in[:, o:o + d_attn]; o += d_attn
    w_qi = w_in[:, o:o + d_qi]; o += d_qi
    w_ki = w_in[:, o:o + IDX_DIM]; o += IDX_DIM
    w_wi = w_in[:, o:o + IDX_HEADS] * (IDX_HEADS * IDX_DIM) ** -0.5
    zeros = jnp.zeros_like(w_ki)
    w_nat = jnp.concatenate([w_k, w_ki, zeros, zeros, w_ki], axis=1).astype(BF16)
    w_qT = jnp.concatenate([w_q.T, w_qi.T], axis=0).astype(BF16)

    x_bf = h2.astype(BF16)
    tm = _tile(n, 1024)
    p_pool = _matmul(x_bf, w_pool.astype(BF16), F32, tm, _tile(d_pool, 512))
    k_nat = _matmul(x_bf, w_nat, BF16, tm, w_nat.shape[1])
    qT_all = _matmul_nt(w_qT, x_bf, BF16, _tile(w_qT.shape[0], 1024), tm)
    vT_blk = _matmul_nt_blocked(w_v.T.astype(BF16), x_bf, BF16, _tile(seq, 512), tq)
    wiT = _matmul_nt(w_wi.T.astype(BF16), x_bf, F32, IDX_HEADS, tm)

    pool_out = _pool_mixer(p_pool, pool_w.astype(BF16), pool_scale.reshape(1, d_pool), seq,
                           _tile(seq, 256))

    attnT = _sparse_attention_split(qT_all, wiT, k_nat, vT_blk, bias, batch, seq, tq, top_k, d_attn)

    w_pT = w_out[:d_pool].T.astype(BF16)
    w_aT = w_out[d_pool:].T.astype(BF16)
    hT, hTb = _out_projection(pool_out, attnT, h2, w_pT, w_aT, ln1_g.reshape(d, 1),
                              ln1_b.reshape(d, 1), alpha, _tile(n, 256))

    sT = _peer_query(hTb, peer_wq.T.astype(BF16), sub_keys.astype(BF16), _tile(n, 256))
    c1, e1, r2, e2 = _peer_route(sT, _tile(n, 256))
    ffnT = _peer_dense(hTb, peer_u.astype(BF16), peer_v.T.astype(BF16), c1, e1, r2, e2,
                       _tile(n, 512), 8 * PEER_KEYS, 4 * PEER_KEYS)
    return _final_ln(hT, ffnT, ln2_g.reshape(d, 1), ln2_b.reshape(d, 1), alpha, _tile(n, 256))


def _sparse_attention_split(qT_all, wiT, k_nat, vT_blk, bias, batch, seq, tq, top_k, d_attn):
    n = batch * seq
    nq = seq // tq
    hd = d_attn // N_HEADS
    d_qi = IDX_HEADS * IDX_DIM
    assert d_qi == d_attn, "q and indexer-q row blocks must have equal height"
    tok0 = lambda b, q: (0, b * nq + q)
    tok1 = lambda b, q: (1, b * nq + q)
    return pl.pallas_call(
        functools.partial(_attn_kernel, top_k=top_k, tq=tq, hd=hd),
        grid=(batch, nq),
        in_specs=[pl.BlockSpec((d_attn, tq), tok0),
                  pl.BlockSpec((d_qi, tq), tok1),
                  pl.BlockSpec((IDX_HEADS, tq), tok0),
                  _resident((seq, d_attn), lambda b, q: (b, 0)),
                  _resident((seq, 4 * IDX_DIM), lambda b, q: (b, d_attn // (4 * IDX_DIM))),
                  _resident((nq, d_attn, tq), lambda b, q: (b, 0, 0)),
                  _resident(bias.shape, lambda b, q: (0, 0, 0, 0))],
        out_specs=pl.BlockSpec((d_attn, tq), tok0),
        out_shape=jax.ShapeDtypeStruct((d_attn, n), BF16),
        scratch_shapes=[pltpu.VMEM((seq, tq), F32),
                        pltpu.VMEM((N_HEADS, 1, tq), F32),
                        pltpu.VMEM((N_HEADS, 1, tq), F32),
                        pltpu.VMEM((N_HEADS, hd, tq), F32),
                        pltpu.VMEM((N_HEADS, tq, tq), F32)],
        compiler_params=_params(2),
    )(qT_all, qT_all, wiT, k_nat, k_nat, vT_blk, bias)


def kernel(x, w_in, pool_w, pool_scale, rel_bias, w_out, ln1_g, ln1_b, peer_wq, peer_subkeys,
           peer_u, peer_v, ln2_g, ln2_b):
    batch, seq, d = x.shape
    depth = w_in.shape[0]
    alpha = (2 * depth) ** 0.25
    tq = min(256, seq)
    bias = _bias_tiles(rel_bias, tq)
    h2 = x.reshape(batch * seq, d)
    for l in range(depth):
        h2 = _layer(h2, batch, seq, w_in[l], pool_w[l], pool_scale[l], bias, w_out[l],
                    ln1_g[l], ln1_b[l], peer_wq[l], peer_subkeys[l], peer_u[l], peer_v[l],
                    ln2_g[l], ln2_b[l], alpha, tq)
    return h2.reshape(batch, seq, d)
```

```python
import functools
import math

import jax
import jax.numpy as jnp
from jax import lax
from jax.experimental import pallas as pl
from jax.experimental.pallas import tpu as pltpu

F32 = jnp.float32
BF16 = jnp.bfloat16
I32 = jnp.int32

LANES = 128
SUBLANES = 8
VMEM_LIMIT = 56 * 1024 * 1024

POOL_WINDOWS = (2, 4, 8, 16)
N_HEADS = 8
IDX_HEADS = 16
IDX_DIM = 64
MAX_TOPK = 256
N_BUCKETS = 32
MAX_DISTANCE = 128
PEER_HEADS = 8
PEER_KEYS = 128
PEER_TOPK = 16
LN_EPS = 1e-5
NEG_INF = -1e30
M_FLOOR = -1e20
BISECT_UNCHECKED = 14
BISECT_CAP = 48
LOG2E = 1.0 / math.log(2.0)


def _params(n_grid):
    return pltpu.CompilerParams(dimension_semantics=("arbitrary",) * n_grid,
                                vmem_limit_bytes=VMEM_LIMIT)


def _resident(shape, index_map):
    return pl.BlockSpec(shape, index_map, pipeline_mode=pl.Buffered(1))


def _proj_kernel(x_ref, wp_ref, wn_ref, wq_ref, wv_ref, ww_ref, pool_ref, nat_ref, qT_ref, vT_ref,
                 wiT_ref):
    x = x_ref[...].astype(BF16)
    nt = (((1,), (1,)), ((), ()))
    pool_ref[...] = jnp.dot(x, wp_ref[...], preferred_element_type=F32)
    nat_ref[...] = jnp.dot(x, wn_ref[...], preferred_element_type=F32).astype(nat_ref.dtype)
    qT_ref[...] = lax.dot_general(wq_ref[...], x, nt, preferred_element_type=F32).astype(qT_ref.dtype)
    vT = lax.dot_general(wv_ref[...], x, nt, preferred_element_type=F32).astype(vT_ref.dtype)
    slab = vT_ref.shape[2]
    for j in range(vT_ref.shape[0]):
        vT_ref[j] = vT[:, j * slab:(j + 1) * slab]
    wiT_ref[...] = lax.dot_general(ww_ref[...], x, nt, preferred_element_type=F32)


def _projections(x2, w_pool, w_nat, w_qT, w_vT, w_wiT, tm, slab):
    n, d = x2.shape
    full = lambda w: _resident(w.shape, lambda i: (0, 0))
    return pl.pallas_call(
        _proj_kernel,
        grid=(n // tm,),
        in_specs=[pl.BlockSpec((tm, d), lambda i: (i, 0)),
                  full(w_pool), full(w_nat), full(w_qT), full(w_vT), full(w_wiT)],
        out_specs=[pl.BlockSpec((tm, w_pool.shape[1]), lambda i: (i, 0)),
                   pl.BlockSpec((tm, w_nat.shape[1]), lambda i: (i, 0)),
                   pl.BlockSpec((w_qT.shape[0], tm), lambda i: (0, i)),
                   pl.BlockSpec((tm // slab, w_vT.shape[0], slab), lambda i: (i, 0, 0)),
                   pl.BlockSpec((w_wiT.shape[0], tm), lambda i: (0, i))],
        out_shape=[jax.ShapeDtypeStruct((n, w_pool.shape[1]), F32),
                   jax.ShapeDtypeStruct((n, w_nat.shape[1]), BF16),
                   jax.ShapeDtypeStruct((w_qT.shape[0], n), BF16),
                   jax.ShapeDtypeStruct((n // slab, w_vT.shape[0], slab), BF16),
                   jax.ShapeDtypeStruct((w_wiT.shape[0], n), F32)],
        compiler_params=_params(1),
    )(x2, w_pool, w_nat, w_qT, w_vT, w_wiT)


def _pool_kernel(v_ref, halo_ref, w_ref, sc_ref, o_ref, buf_ref, *, seq_tiles, halo):
    ts = v_ref.shape[0]
    group = w_ref.shape[1]
    tile = pl.program_id(0) % seq_tiles
    buf_ref[0:halo, :] = jnp.where(tile == 0, 0.0, halo_ref[...])
    buf_ref[halo:, :] = v_ref[...]
    pos = tile * ts + lax.broadcasted_iota(I32, (ts, 1), 0)
    for g, win in enumerate(POOL_WINDOWS):
        cols = slice(g * group, (g + 1) * group)
        tok = buf_ref[halo:halo + ts, cols]
        acc = tok
        for j in range(1, win):
            acc = acc + buf_ref[halo - j:halo - j + ts, cols]
        cnt = jnp.minimum(pos + 1, win).astype(F32)
        pooled = acc / cnt - tok
        mixed = jnp.dot(pooled.astype(BF16), w_ref[g], preferred_element_type=F32)
        o_ref[:, cols] = (mixed * sc_ref[:, cols]).astype(o_ref.dtype)


def _pool_mixer(p_pool, pool_w, pool_scale, seq, ts):
    n, d_pool = p_pool.shape
    halo = max(POOL_WINDOWS)
    seq_tiles = seq // ts
    return pl.pallas_call(
        functools.partial(_pool_kernel, seq_tiles=seq_tiles, halo=halo),
        grid=(n // ts,),
        in_specs=[pl.BlockSpec((ts, d_pool), lambda r: (r, 0)),
                  pl.BlockSpec((halo, d_pool), lambda r: (jnp.maximum(r * (ts // halo) - 1, 0), 0)),
                  pl.BlockSpec(pool_w.shape, lambda r: (0, 0, 0)),
                  pl.BlockSpec((1, d_pool), lambda r: (0, 0))],
        out_specs=pl.BlockSpec((ts, d_pool), lambda r: (r, 0)),
        out_shape=jax.ShapeDtypeStruct((n, d_pool), BF16),
        scratch_shapes=[pltpu.VMEM((halo + ts, d_pool), F32)],
        compiler_params=_params(1),
    )(p_pool, p_pool, pool_w, pool_scale)


def _bias_kernel(rb_ref, o_ref, *, tq):
    n_heads, n_slabs, ks, _ = o_ref.shape
    max_exact = N_BUCKETS // 2
    for j in range(n_slabs):
        dist = (lax.broadcasted_iota(I32, (ks, tq), 1) + (tq - ks * j)
                - lax.broadcasted_iota(I32, (ks, tq), 0))
        dist = jnp.maximum(dist, 0)
        d = jnp.maximum(dist, 1).astype(F32)
        large = max_exact + (jnp.log(d / max_exact) / math.log(MAX_DISTANCE / max_exact)
                             * (N_BUCKETS - max_exact))
        coord = jnp.where(dist < max_exact, dist.astype(F32), large)
        for h in range(n_heads):
            def pick(b, acc, h=h, coord=coord):
                return jnp.where(coord >= jnp.asarray(b, F32), rb_ref[b, h], acc)
            bias = lax.fori_loop(1, N_BUCKETS, pick, jnp.full((ks, tq), rb_ref[0, h], F32))
            o_ref[h, j] = (bias - rb_ref[N_BUCKETS - 1, h]) * LOG2E


def _bias_tiles(rel_bias, tq):
    n_heads = rel_bias.shape[1]
    return pl.pallas_call(
        functools.partial(_bias_kernel, tq=tq),
        in_specs=[pl.BlockSpec(memory_space=pltpu.SMEM)],
        out_specs=pl.BlockSpec(memory_space=pltpu.VMEM),
        out_shape=jax.ShapeDtypeStruct((n_heads, 2, tq, tq), F32),
        compiler_params=pltpu.CompilerParams(vmem_limit_bytes=VMEM_LIMIT),
    )(rel_bias)


def _attn_kernel(qT_ref, qiT_ref, wiT_ref, k_ref, ki2_ref, vT_ref, bias_ref, o_ref,
                 key_ref, m_ref, l_ref, acc_ref, s_ref, *, top_k, tq, hd):
    ks = LANES
    halves = tq // ks
    qb = pl.program_id(1)
    n_blk = qb + 1
    q_pos = qb * tq + lax.broadcasted_iota(I32, (1, tq), 1)

    def key_pos(r0):
        return r0 + lax.broadcasted_iota(I32, (ks, 1), 0)

    w_all = wiT_ref[...]

    def score_blk(kb, carry):
        lo, hi = carry
        for hf in range(halves):
            r0 = pl.multiple_of(kb * tq + hf * ks, ks)
            acc = jnp.zeros((ks, tq), F32)
            for p in range(IDX_HEADS // 2):
                qp = qiT_ref[p * LANES:(p + 1) * LANES, :]
                z0 = jnp.dot(ki2_ref[pl.ds(r0, ks), 0:LANES], qp, preferred_element_type=F32)
                z1 = jnp.dot(ki2_ref[pl.ds(r0, ks), LANES:2 * LANES], qp,
                             preferred_element_type=F32)
                acc = acc + w_all[2 * p:2 * p + 1, :] * jnp.maximum(z0, 0.0)
                acc = acc + w_all[2 * p + 1:2 * p + 2, :] * jnp.maximum(z1, 0.0)
            causal = key_pos(r0) <= q_pos
            key_ref[pl.ds(r0, ks), :] = jnp.where(causal, acc, NEG_INF)
            hi = jnp.maximum(hi, jnp.max(jnp.where(causal, acc, -jnp.inf), axis=0, keepdims=True))
            lo = jnp.minimum(lo, jnp.min(jnp.where(causal, acc, jnp.inf), axis=0, keepdims=True))
        return lo, hi

    lo, hi = lax.fori_loop(0, n_blk, score_blk,
                           (jnp.full((1, tq), jnp.inf, F32), jnp.full((1, tq), -jnp.inf, F32)))

    def pending(state):
        i, _, _, cnt_lo = state
        return jnp.logical_and(i < BISECT_CAP, jnp.max(cnt_lo) > top_k)

    def bisect(state):
        i, lo, hi, cnt_lo = state
        mid = lo + 0.5 * (hi - lo)

        def count_blk(kb, c8):
            for hf in range(halves):
                r0 = pl.multiple_of(kb * tq + hf * ks, ks)
                hit = (key_ref[pl.ds(r0, ks), :] >= mid).astype(F32)
                c8 = c8 + jnp.sum(hit.reshape(ks // SUBLANES, SUBLANES, tq), axis=0)
            return c8

        c8 = lax.fori_loop(0, n_blk, count_blk, jnp.zeros((SUBLANES, tq), F32))
        cnt = jnp.sum(c8, axis=0, keepdims=True)
        take = cnt >= top_k
        return (i + 1, jnp.where(take, mid, lo), jnp.where(take, hi, mid),
                jnp.where(take, cnt, cnt_lo))

    state = (jnp.int32(0), lo, hi, (q_pos + 1).astype(F32))
    state = lax.fori_loop(0, BISECT_UNCHECKED, lambda _, s: bisect(s), state)
    _, thr, _, _ = lax.while_loop(pending, bisect, state)

    def mask_blk(kb, carry):
        for hf in range(halves):
            r0 = pl.multiple_of(kb * tq + hf * ks, ks)
            sel = (key_ref[pl.ds(r0, ks), :] >= thr) & (key_pos(r0) <= q_pos)
            key_ref[pl.ds(r0, ks), :] = jnp.where(sel, 0.0, NEG_INF).astype(F32)
        return carry

    lax.fori_loop(0, n_blk, mask_blk, 0)

    m_ref[...] = jnp.full(m_ref.shape, M_FLOOR, F32)
    l_ref[...] = jnp.zeros(l_ref.shape, F32)
    acc_ref[...] = jnp.zeros(acc_ref.shape, F32)

    def step(kb, carry, with_bias):
        r0 = pl.multiple_of(kb * tq, tq)
        mask = key_ref[pl.ds(r0, tq), :]
        maxes = []
        for h in range(N_HEADS):
            hs = slice(h * hd, (h + 1) * hd)
            s = jnp.dot(k_ref[pl.ds(r0, tq), hs], qT_ref[hs, :], preferred_element_type=F32) + mask
            if with_bias:
                s = s + bias_ref[h, kb - (qb - 1)]
            s_ref[h] = s
            maxes.append(jnp.max(s, axis=0, keepdims=True))
        for h in range(N_HEADS):
            hs = slice(h * hd, (h + 1) * hd)
            m = m_ref[h]
            m_new = jnp.maximum(m, maxes[h])
            alpha = jnp.exp2(m - m_new)
            p = jnp.exp2(s_ref[h] - m_new)
            l_ref[h] = alpha * l_ref[h] + jnp.sum(p, axis=0, keepdims=True)
            acc_ref[h] = alpha * acc_ref[h] + jnp.dot(vT_ref[kb, hs, :], p.astype(BF16),
                                                      preferred_element_type=F32)
            m_ref[h] = m_new
        return carry

    near0 = jnp.maximum(qb - 1, 0)
    lax.fori_loop(0, near0, functools.partial(step, with_bias=False), 0)
    lax.fori_loop(near0, n_blk, functools.partial(step, with_bias=True), 0)
    for h in range(N_HEADS):
        o_ref[h * hd:(h + 1) * hd, :] = (acc_ref[h] / l_ref[h]).astype(o_ref.dtype)


def _layer_norm_t(r, g, b):
    mu = jnp.mean(r, axis=0, keepdims=True)
    c = r - mu
    var = jnp.mean(c * c, axis=0, keepdims=True)
    return c * lax.rsqrt(var + LN_EPS) * g + b


def _outproj_kernel(pool_ref, attnT_ref, x_ref, wpT_ref, waT_ref, g_ref, b_ref, hT_ref, hTb_ref,
                    *, alpha):
    mix = lax.dot_general(wpT_ref[...], pool_ref[...], (((1,), (1,)), ((), ())),
                          preferred_element_type=F32)
    mix = mix + jnp.dot(waT_ref[...], attnT_ref[...], preferred_element_type=F32)
    h = _layer_norm_t(alpha * x_ref[...].T + mix, g_ref[...], b_ref[...])
    hT_ref[...] = h
    hTb_ref[...] = h.astype(BF16)


def _out_projection(pool_out, attnT, x2, wpT, waT, g, b, alpha, tm):
    n, d = x2.shape
    d_pool = pool_out.shape[1]
    d_attn = attnT.shape[0]
    return pl.pallas_call(
        functools.partial(_outproj_kernel, alpha=alpha),
        grid=(n // tm,),
        in_specs=[pl.BlockSpec((tm, d_pool), lambda i: (i, 0)),
                  pl.BlockSpec((d_attn, tm), lambda i: (0, i)),
                  pl.BlockSpec((tm, d), lambda i: (i, 0)),
                  _resident((d, d_pool), lambda i: (0, 0)),
                  _resident((d, d_attn), lambda i: (0, 0)),
                  _resident((d, 1), lambda i: (0, 0)),
                  _resident((d, 1), lambda i: (0, 0))],
        out_specs=[pl.BlockSpec((d, tm), lambda i: (0, i)),
                   pl.BlockSpec((d, tm), lambda i: (0, i))],
        out_shape=[jax.ShapeDtypeStruct((d, n), F32), jax.ShapeDtypeStruct((d, n), BF16)],
        compiler_params=_params(1),
    )(pool_out, attnT, x2, wpT, waT, g, b)


def _peer_query_kernel(hTb_ref, wqT_ref, sk_ref, sT_ref):
    qhT = jnp.dot(wqT_ref[...], hTb_ref[...], preferred_element_type=F32).astype(BF16)
    half = sk_ref.shape[2]
    for c in range(sT_ref.shape[0]):
        sT_ref[c] = jnp.dot(sk_ref[c % 2], qhT[c * half:(c + 1) * half, :],
                            preferred_element_type=F32)


def _peer_query(hTb, wqT, sub_keys, tm):
    d, n = hTb.shape
    dq = wqT.shape[0]
    n_keys, half = sub_keys.shape[1:]
    n_slices = dq // half
    return pl.pallas_call(
        _peer_query_kernel,
        grid=(n // tm,),
        in_specs=[pl.BlockSpec((d, tm), lambda i: (0, i)),
                  _resident((dq, d), lambda i: (0, 0)),
                  _resident(sub_keys.shape, lambda i: (0, 0, 0))],
        out_specs=pl.BlockSpec((n_slices, n_keys, tm), lambda i: (0, 0, i)),
        out_shape=jax.ShapeDtypeStruct((n_slices, n_keys, n), F32),
        compiler_params=_params(1),
    )(hTb, wqT, sub_keys)


def _bitonic_sort_desc(xs):
    xs = list(xs)
    n = len(xs)
    k = 2
    while k <= n:
        j = k // 2
        while j >= 1:
            for i in range(n):
                o = i ^ j
                if o > i:
                    hi, lo = jnp.maximum(xs[i], xs[o]), jnp.minimum(xs[i], xs[o])
                    xs[i], xs[o] = (hi, lo) if (i & k) == 0 else (lo, hi)
            j //= 2
        k *= 2
    return xs


def _bitonic_merge_desc(xs):
    xs = list(xs)
    n = len(xs)
    j = n // 2
    while j >= 1:
        for i in range(n):
            o = i ^ j
            if o > i:
                xs[i], xs[o] = jnp.maximum(xs[i], xs[o]), jnp.minimum(xs[i], xs[o])
        j //= 2
    return xs


def _top_of_two_sorted(a, b):
    n = len(a)
    return [jnp.maximum(a[i], b[n - 1 - i]) for i in range(n)]


def _top16_over_keys(s):
    groups = s.shape[0] // SUBLANES
    xs = _bitonic_sort_desc([s[g * SUBLANES:(g + 1) * SUBLANES, :] for g in range(groups)])
    shift = SUBLANES // 2
    while shift >= 1:
        other = [pltpu.roll(a, shift, 0) for a in xs]
        xs = _bitonic_merge_desc(_top_of_two_sorted(xs, other))
        shift //= 2
    return xs


def _peer_route_kernel(sT_ref, c1_ref, e1_ref, r2_ref, e2_ref):
    k = PEER_TOPK
    assert PEER_HEADS == SUBLANES, "one head per sublane in the candidate stage"
    sub = lax.broadcasted_iota(I32, (SUBLANES, sT_ref.shape[2]), 0)
    a, b = None, None
    for h in range(PEER_HEADS):
        ah = _top16_over_keys(sT_ref[2 * h])
        bh = _top16_over_keys(sT_ref[2 * h + 1])
        a = ah if a is None else [jnp.where(sub == h, x, y) for x, y in zip(ah, a)]
        b = bh if b is None else [jnp.where(sub == h, x, y) for x, y in zip(bh, b)]
    pairs = [(i, j) for i in range(k) for j in range(k) if (i + 1) * (j + 1) <= k]
    cands = [a[i] + b[j] for i, j in pairs]
    pad = jnp.full_like(cands[0], -jnp.inf)
    padded = cands + [pad] * (-len(cands) % k)
    top = _bitonic_sort_desc(padded[:k])
    for g in range(1, len(padded) // k):
        nxt = _bitonic_sort_desc(padded[g * k:(g + 1) * k])
        top = _top_of_two_sorted(top, nxt)
        if g + 1 < len(padded) // k:
            top = _bitonic_merge_desc(top)
    cmax = a[0] + b[0]
    thr = functools.reduce(jnp.minimum, top)
    z = functools.reduce(jnp.add, [jnp.exp(t - cmax) for t in top])
    log2z = jnp.log(z) * LOG2E
    cut = []
    for r in range(k):
        hits = [jnp.where(c >= thr, 1.0, 0.0) for c, (i, j) in zip(cands, pairs) if i == r]
        cut.append(functools.reduce(jnp.add, hits) - 1.0)
    for h in range(PEER_HEADS):
        row = slice(h, h + 1)
        s1 = sT_ref[2 * h]
        s2 = sT_ref[2 * h + 1]
        c1 = jnp.full(s1.shape, -1.0, F32)
        r2 = jnp.full(s2.shape, float(k), F32)
        for r in reversed(range(k)):
            c1 = jnp.where(s1 == a[r][row, :], cut[r][row, :], c1)
            r2 = jnp.where(s2 == b[r][row, :], float(r), r2)
        c1_ref[h] = c1
        r2_ref[h] = r2.astype(BF16)
        e1_ref[h] = jnp.exp2((s1 - a[0][row, :]) * LOG2E)
        e2_ref[h] = jnp.exp2((s2 - b[0][row, :]) * LOG2E - log2z[row, :]).astype(BF16)


def _peer_route(sT, tt):
    n_slices, n_keys, n = sT.shape
    big = pl.BlockSpec((PEER_HEADS, n_keys, tt), lambda i: (0, 0, i))
    shape = (PEER_HEADS, n_keys, n)
    return pl.pallas_call(
        _peer_route_kernel,
        grid=(n // tt,),
        in_specs=[pl.BlockSpec((n_slices, n_keys, tt), lambda i: (0, 0, i))],
        out_specs=[big, big, big, big],
        out_shape=[jax.ShapeDtypeStruct(shape, F32), jax.ShapeDtypeStruct(shape, F32),
                   jax.ShapeDtypeStruct(shape, BF16), jax.ShapeDtypeStruct(shape, BF16)],
        compiler_params=_params(1),
    )(sT)


def _peer_dense_kernel(hTb_ref, u_ref, vT_ref, c1_ref, e1_ref, r2_ref, e2_ref, o_ref, *, chunk):
    ei = pl.program_id(1)
    te = u_ref.shape[0]
    rows = te // PEER_KEYS
    rows_per_chunk = chunk // PEER_KEYS

    @pl.when(ei == 0)
    def _():
        o_ref[...] = jnp.zeros_like(o_ref)

    i0 = pl.multiple_of(ei * rows, rows)

    def pre_act(c):
        return jnp.dot(u_ref[c * chunk:(c + 1) * chunk, :], hTb_ref[...],
                       preferred_element_type=F32)

    n_chunks = te // chunk
    act_next = pre_act(0)
    for c in range(n_chunks):
        act = act_next
        if c + 1 < n_chunks:
            act_next = pre_act(c + 1)
        gates = []
        for rr in range(rows_per_chunk):
            r = c * rows_per_chunk + rr
            gate = jnp.zeros((PEER_KEYS, act.shape[1]), BF16)
            for h in range(PEER_HEADS):
                cut = c1_ref[h, pl.ds(i0, rows), :][r:r + 1, :].astype(BF16)
                e1 = e1_ref[h, pl.ds(i0, rows), :][r:r + 1, :].astype(BF16)
                gate = gate + jnp.where(r2_ref[h] <= cut, e2_ref[h] * e1, jnp.zeros_like(gate))
            gates.append(gate)
        act = 0.5 * act * (1.0 + lax.erf(act * math.sqrt(0.5)))
        p = jnp.concatenate(
            [gates[rr] * act[rr * PEER_KEYS:(rr + 1) * PEER_KEYS, :].astype(BF16)
             for rr in range(rows_per_chunk)], axis=0)
        o_ref[...] += jnp.dot(vT_ref[:, c * chunk:(c + 1) * chunk], p, preferred_element_type=F32)


def _peer_dense(hTb, u, vT, c1, e1, r2, e2, tt, te, chunk):
    d, n = hTb.shape
    n_exp = u.shape[0]
    n_keys = c1.shape[1]
    big = pl.BlockSpec((PEER_HEADS, n_keys, tt), lambda t, e: (0, 0, t))
    return pl.pallas_call(
        functools.partial(_peer_dense_kernel, chunk=chunk),
        grid=(n // tt, n_exp // te),
        in_specs=[pl.BlockSpec((d, tt), lambda t, e: (0, t)),
                  pl.BlockSpec((te, d), lambda t, e: (e, 0)),
                  pl.BlockSpec((d, te), lambda t, e: (0, e)),
                  big, big, big, big],
        out_specs=pl.BlockSpec((d, tt), lambda t, e: (0, t)),
        out_shape=jax.ShapeDtypeStruct((d, n), F32),
        compiler_params=_params(2),
    )(hTb, u, vT, c1, e1, r2, e2)


def _final_ln_kernel(hT_ref, fT_ref, g_ref, b_ref, o_ref, *, alpha):
    y = _layer_norm_t(alpha * hT_ref[...] + fT_ref[...], g_ref[...], b_ref[...])
    o_ref[...] = y.T


def _final_ln(hT, ffnT, g, b, alpha, tm):
    d, n = hT.shape
    return pl.pallas_call(
        functools.partial(_final_ln_kernel, alpha=alpha),
        grid=(n // tm,),
        in_specs=[pl.BlockSpec((d, tm), lambda i: (0, i)),
                  pl.BlockSpec((d, tm), lambda i: (0, i)),
                  _resident((d, 1), lambda i: (0, 0)),
                  _resident((d, 1), lambda i: (0, 0))],
        out_specs=pl.BlockSpec((tm, d), lambda i: (i, 0)),
        out_shape=jax.ShapeDtypeStruct((n, d), F32),
        compiler_params=_params(1),
    )(hT, ffnT, g, b)


def _tile(n, pref):
    t = min(pref, n)
    while n % t:
        t -= LANES
    return t


def _layer(h2, batch, seq, w_in, pool_w, pool_scale, bias, w_out, ln1_g, ln1_b,
           peer_wq, sub_keys, peer_u, peer_v, ln2_g, ln2_b, alpha, tq):
    n, d = h2.shape
    d_pool = pool_w.shape[0] * pool_w.shape[1]
    d_attn = w_out.shape[0] - d_pool
    hd = d_attn // N_HEADS
    d_qi = IDX_HEADS * IDX_DIM
    top_k = min(MAX_TOPK, seq // 4)

    o = 0
    w_pool = w_in[:, o:o + d_pool]; o += d_pool
    w_q = w_in[:, o:o + d_attn] * (hd ** -0.5 * LOG2E); o += d_attn
    w_k = w_in[:, o:o + d_attn]; o += d_attn
    w_v = w_in[:, o:o + d_attn]; o += d_attn
    w_qi = w_in[:, o:o + d_qi]; o += d_qi
    w_ki = w_in[:, o:o + IDX_DIM]; o += IDX_DIM
    w_wi = w_in[:, o:o + IDX_HEADS] * (IDX_HEADS * IDX_DIM) ** -0.5
    zeros = jnp.zeros_like(w_ki)
    w_nat = jnp.concatenate([w_k, w_ki, zeros, zeros, w_ki], axis=1).astype(BF16)
    w_qT = jnp.concatenate([w_q.T, w_qi.T], axis=0).astype(BF16)

    p_pool, k_nat, qT_all, vT_blk, wiT = _projections(
        h2, w_pool.astype(BF16), w_nat, w_qT, w_v.T.astype(BF16), w_wi.T.astype(BF16),
        _tile(seq, 512), tq)

    pool_out = _pool_mixer(p_pool, pool_w.astype(BF16), pool_scale.reshape(1, d_pool), seq,
                           _tile(seq, 256))

    attnT = _sparse_attention_split(qT_all, wiT, k_nat, vT_blk, bias, batch, seq, tq, top_k, d_attn)

    w_pT = w_out[:d_pool].T.astype(BF16)
    w_aT = w_out[d_pool:].T.astype(BF16)
    hT, hTb = _out_projection(pool_out, attnT, h2, w_pT, w_aT, ln1_g.reshape(d, 1),
                              ln1_b.reshape(d, 1), alpha, _tile(n, 256))

    sT = _peer_query(hTb, peer_wq.T.astype(BF16), sub_keys.astype(BF16), _tile(n, 256))
    c1, e1, r2, e2 = _peer_route(sT, _tile(n, 256))
    ffnT = _peer_dense(hTb, peer_u.astype(BF16), peer_v.T.astype(BF16), c1, e1, r2, e2,
                       _tile(n, 512), 8 * PEER_KEYS, 4 * PEER_KEYS)
    return _final_ln(hT, ffnT, ln2_g.reshape(d, 1), ln2_b.reshape(d, 1), alpha, _tile(n, 256))


def _sparse_attention_split(qT_all, wiT, k_nat, vT_blk, bias, batch, seq, tq, top_k, d_attn):
    n = batch * seq
    nq = seq // tq
    hd = d_attn // N_HEADS
    d_qi = IDX_HEADS * IDX_DIM
    assert d_qi == d_attn, "q and indexer-q row blocks must have equal height"
    tok0 = lambda b, q: (0, b * nq + q)
    tok1 = lambda b, q: (1, b * nq + q)
    return pl.pallas_call(
        functools.partial(_attn_kernel, top_k=top_k, tq=tq, hd=hd),
        grid=(batch, nq),
        in_specs=[pl.BlockSpec((d_attn, tq), tok0),
                  pl.BlockSpec((d_qi, tq), tok1),
                  pl.BlockSpec((IDX_HEADS, tq), tok0),
                  _resident((seq, d_attn), lambda b, q: (b, 0)),
                  _resident((seq, 4 * IDX_DIM), lambda b, q: (b, d_attn // (4 * IDX_DIM))),
                  _resident((nq, d_attn, tq), lambda b, q: (b, 0, 0)),
                  _resident(bias.shape, lambda b, q: (0, 0, 0, 0))],
        out_specs=pl.BlockSpec((d_attn, tq), tok0),
        out_shape=jax.ShapeDtypeStruct((d_attn, n), BF16),
        scratch_shapes=[pltpu.VMEM((seq, tq), F32),
                        pltpu.VMEM((N_HEADS, 1, tq), F32),
                        pltpu.VMEM((N_HEADS, 1, tq), F32),
                        pltpu.VMEM((N_HEADS, hd, tq), F32),
                        pltpu.VMEM((N_HEADS, tq, tq), F32)],
        compiler_params=_params(2),
    )(qT_all, qT_all, wiT, k_nat, k_nat, vT_blk, bias)


def kernel(x, w_in, pool_w, pool_scale, rel_bias, w_out, ln1_g, ln1_b, peer_wq, peer_subkeys,
           peer_u, peer_v, ln2_g, ln2_b):
    batch, seq, d = x.shape
    depth = w_in.shape[0]
    alpha = (2 * depth) ** 0.25
    tq = min(256, seq)
    bias = _bias_tiles(rel_bias, tq)
    h2 = x.reshape(batch * seq, d)
    for l in range(depth):
        h2 = _layer(h2, batch, seq, w_in[l], pool_w[l], pool_scale[l], bias, w_out[l],
                    ln1_g[l], ln1_b[l], peer_wq[l], peer_subkeys[l], peer_u[l], peer_v[l],
                    ln2_g[l], ln2_b[l], alpha, tq)
    return h2.reshape(batch, seq, d)
```

```python
import functools
import math

import jax
import jax.numpy as jnp
from jax import lax
from jax.experimental import pallas as pl
from jax.experimental.pallas import tpu as pltpu

F32 = jnp.float32
BF16 = jnp.bfloat16
I32 = jnp.int32

LANES = 128
SUBLANES = 8
VMEM_LIMIT = 56 * 1024 * 1024

POOL_WINDOWS = (2, 4, 8, 16)
N_HEADS = 8
IDX_HEADS = 16
IDX_DIM = 64
MAX_TOPK = 256
N_BUCKETS = 32
MAX_DISTANCE = 128
PEER_HEADS = 8
PEER_KEYS = 128
PEER_TOPK = 16
LN_EPS = 1e-5
NEG_INF = -1e30
M_FLOOR = -1e20
BISECT_UNCHECKED = 14
BISECT_CAP = 48
LOG2E = 1.0 / math.log(2.0)


def _params(n_grid):
    return pltpu.CompilerParams(dimension_semantics=("arbitrary",) * n_grid,
                                vmem_limit_bytes=VMEM_LIMIT)


def _resident(shape, index_map):
    return pl.BlockSpec(shape, index_map, pipeline_mode=pl.Buffered(1))


def _proj_kernel(x_ref, wp_ref, wn_ref, wq_ref, wv_ref, ww_ref, pool_ref, nat_ref, qT_ref, vT_ref,
                 wiT_ref):
    x = x_ref[...].astype(BF16)
    nt = (((1,), (1,)), ((), ()))
    pool_ref[...] = jnp.dot(x, wp_ref[...], preferred_element_type=F32)
    nat_ref[...] = jnp.dot(x, wn_ref[...], preferred_element_type=F32).astype(nat_ref.dtype)
    qT_ref[...] = lax.dot_general(wq_ref[...], x, nt, preferred_element_type=F32).astype(qT_ref.dtype)
    vT = lax.dot_general(wv_ref[...], x, nt, preferred_element_type=F32).astype(vT_ref.dtype)
    slab = vT_ref.shape[2]
    for j in range(vT_ref.shape[0]):
        vT_ref[j] = vT[:, j * slab:(j + 1) * slab]
    wiT_ref[...] = lax.dot_general(ww_ref[...], x, nt, preferred_element_type=F32)


def _projections(x2, w_pool, w_nat, w_qT, w_vT, w_wiT, tm, slab):
    n, d = x2.shape
    full = lambda w: _resident(w.shape, lambda i: (0, 0))
    return pl.pallas_call(
        _proj_kernel,
        grid=(n // tm,),
        in_specs=[pl.BlockSpec((tm, d), lambda i: (i, 0)),
                  full(w_pool), full(w_nat), full(w_qT), full(w_vT), full(w_wiT)],
        out_specs=[pl.BlockSpec((tm, w_pool.shape[1]), lambda i: (i, 0)),
                   pl.BlockSpec((tm, w_nat.shape[1]), lambda i: (i, 0)),
                   pl.BlockSpec((w_qT.shape[0], tm), lambda i: (0, i)),
                   pl.BlockSpec((tm // slab, w_vT.shape[0], slab), lambda i: (i, 0, 0)),
                   pl.BlockSpec((w_wiT.shape[0], tm), lambda i: (0, i))],
        out_shape=[jax.ShapeDtypeStruct((n, w_pool.shape[1]), F32),
                   jax.ShapeDtypeStruct((n, w_nat.shape[1]), BF16),
                   jax.ShapeDtypeStruct((w_qT.shape[0], n), BF16),
                   jax.ShapeDtypeStruct((n // slab, w_vT.shape[0], slab), BF16),
                   jax.ShapeDtypeStruct((w_wiT.shape[0], n), F32)],
        compiler_params=_params(1),
    )(x2, w_pool, w_nat, w_qT, w_vT, w_wiT)


def _pool_kernel(v_ref, halo_ref, w_ref, sc_ref, o_ref, buf_ref, *, seq_tiles, halo):
    ts = v_ref.shape[0]
    group = w_ref.shape[1]
    tile = pl.program_id(0) % seq_tiles
    buf_ref[0:halo, :] = jnp.where(tile == 0, 0.0, halo_ref[...])
    buf_ref[halo:, :] = v_ref[...]
    pos = tile * ts + lax.broadcasted_iota(I32, (ts, 1), 0)
    for g, win in enumerate(POOL_WINDOWS):
        cols = slice(g * group, (g + 1) * group)
        tok = buf_ref[halo:halo + ts, cols]
        acc = tok
        for j in range(1, win):
            acc = acc + buf_ref[halo - j:halo - j + ts, cols]
        cnt = jnp.minimum(pos + 1, win).astype(F32)
        pooled = acc / cnt - tok
        mixed = jnp.dot(pooled.astype(BF16), w_ref[g], preferred_element_type=F32)
        o_ref[:, cols] = (mixed * sc_ref[:, cols]).astype(o_ref.dtype)


def _pool_mixer(p_pool, pool_w, pool_scale, seq, ts):
    n, d_pool = p_pool.shape
    halo = max(POOL_WINDOWS)
    seq_tiles = seq // ts
    return pl.pallas_call(
        functools.partial(_pool_kernel, seq_tiles=seq_tiles, halo=halo),
        grid=(n // ts,),
        in_specs=[pl.BlockSpec((ts, d_pool), lambda r: (r, 0)),
                  pl.BlockSpec((halo, d_pool), lambda r: (jnp.maximum(r * (ts // halo) - 1, 0), 0)),
                  pl.BlockSpec(pool_w.shape, lambda r: (0, 0, 0)),
                  pl.BlockSpec((1, d_pool), lambda r: (0, 0))],
        out_specs=pl.BlockSpec((ts, d_pool), lambda r: (r, 0)),
        out_shape=jax.ShapeDtypeStruct((n, d_pool), BF16),
        scratch_shapes=[pltpu.VMEM((halo + ts, d_pool), F32)],
        compiler_params=_params(1),
    )(p_pool, p_pool, pool_w, pool_scale)


def _bias_kernel(rb_ref, o_ref, *, tq):
    n_heads, n_slabs, ks, _ = o_ref.shape
    max_exact = N_BUCKETS // 2
    for j in range(n_slabs):
        dist = (lax.broadcasted_iota(I32, (ks, tq), 1) + (tq - ks * j)
                - lax.broadcasted_iota(I32, (ks, tq), 0))
        dist = jnp.maximum(dist, 0)
        d = jnp.maximum(dist, 1).astype(F32)
        large = max_exact + (jnp.log(d / max_exact) / math.log(MAX_DISTANCE / max_exact)
                             * (N_BUCKETS - max_exact))
        coord = jnp.where(dist < max_exact, dist.astype(F32), large)
        for h in range(n_heads):
            def pick(b, acc, h=h, coord=coord):
                return jnp.where(coord >= jnp.asarray(b, F32), rb_ref[b, h], acc)
            bias = lax.fori_loop(1, N_BUCKETS, pick, jnp.full((ks, tq), rb_ref[0, h], F32))
            o_ref[h, j] = (bias - rb_ref[N_BUCKETS - 1, h]) * LOG2E


def _bias_tiles(rel_bias, tq):
    n_heads = rel_bias.shape[1]
    return pl.pallas_call(
        functools.partial(_bias_kernel, tq=tq),
        in_specs=[pl.BlockSpec(memory_space=pltpu.SMEM)],
        out_specs=pl.BlockSpec(memory_space=pltpu.VMEM),
        out_shape=jax.ShapeDtypeStruct((n_heads, 2, tq, tq), F32),
        compiler_params=pltpu.CompilerParams(vmem_limit_bytes=VMEM_LIMIT),
    )(rel_bias)


def _attn_kernel(qT_ref, qiT_ref, wiT_ref, k_ref, ki2_ref, vT_ref, bias_ref, o_ref,
                 key_ref, m_ref, l_ref, acc_ref, s_ref, *, top_k, tq, hd):
    ks = LANES
    halves = tq // ks
    qb = pl.program_id(1)
    n_blk = qb + 1
    q_pos = qb * tq + lax.broadcasted_iota(I32, (1, tq), 1)

    def key_pos(r0):
        return r0 + lax.broadcasted_iota(I32, (ks, 1), 0)

    w_all = wiT_ref[...]

    def score_blk(kb, carry):
        lo, hi = carry
        for hf in range(halves):
            r0 = pl.multiple_of(kb * tq + hf * ks, ks)
            acc = jnp.zeros((ks, tq), F32)
            for p in range(IDX_HEADS // 2):
                qp = qiT_ref[p * LANES:(p + 1) * LANES, :]
                z0 = jnp.dot(ki2_ref[pl.ds(r0, ks), 0:LANES], qp, preferred_element_type=F32)
                z1 = jnp.dot(ki2_ref[pl.ds(r0, ks), LANES:2 * LANES], qp,
                             preferred_element_type=F32)
                acc = acc + w_all[2 * p:2 * p + 1, :] * jnp.maximum(z0, 0.0)
                acc = acc + w_all[2 * p + 1:2 * p + 2, :] * jnp.maximum(z1, 0.0)
            causal = key_pos(r0) <= q_pos
            key_ref[pl.ds(r0, ks), :] = jnp.where(causal, acc, NEG_INF)
            hi = jnp.maximum(hi, jnp.max(jnp.where(causal, acc, -jnp.inf), axis=0, keepdims=True))
            lo = jnp.minimum(lo, jnp.min(jnp.where(causal, acc, jnp.inf), axis=0, keepdims=True))
        return lo, hi

    lo, hi = lax.fori_loop(0, n_blk, score_blk,
                           (jnp.full((1, tq), jnp.inf, F32), jnp.full((1, tq), -jnp.inf, F32)))

    def pending(state):
        i, _, _, cnt_lo = state
        return jnp.logical_and(i < BISECT_CAP, jnp.max(cnt_lo) > top_k)

    def bisect(state):
        i, lo, hi, cnt_lo = state
        mid = lo + 0.5 * (hi - lo)

        def count_blk(kb, c8):
            for hf in range(halves):
                r0 = pl.multiple_of(kb * tq + hf * ks, ks)
                hit = (key_ref[pl.ds(r0, ks), :] >= mid).astype(F32)
                c8 = c8 + jnp.sum(hit.reshape(ks // SUBLANES, SUBLANES, tq), axis=0)
            return c8

        c8 = lax.fori_loop(0, n_blk, count_blk, jnp.zeros((SUBLANES, tq), F32))
        cnt = jnp.sum(c8, axis=0, keepdims=True)
        take = cnt >= top_k
        return (i + 1, jnp.where(take, mid, lo), jnp.where(take, hi, mid),
                jnp.where(take, cnt, cnt_lo))

    state = (jnp.int32(0), lo, hi, (q_pos + 1).astype(F32))
    state = lax.fori_loop(0, BISECT_UNCHECKED, lambda _, s: bisect(s), state)
    _, thr, _, _ = lax.while_loop(pending, bisect, state)

    def mask_blk(kb, carry):
        for hf in range(halves):
            r0 = pl.multiple_of(kb * tq + hf * ks, ks)
            sel = (key_ref[pl.ds(r0, ks), :] >= thr) & (key_pos(r0) <= q_pos)
            key_ref[pl.ds(r0, ks), :] = jnp.where(sel, 0.0, NEG_INF).astype(F32)
        return carry

    lax.fori_loop(0, n_blk, mask_blk, 0)

    m_ref[...] = jnp.full(m_ref.shape, M_FLOOR, F32)
    l_ref[...] = jnp.zeros(l_ref.shape, F32)
    acc_ref[...] = jnp.zeros(acc_ref.shape, F32)

    def step(kb, carry, with_bias):
        r0 = pl.multiple_of(kb * tq, tq)
        mask = key_ref[pl.ds(r0, tq), :]
        maxes = []
        for h in range(N_HEADS):
            hs = slice(h * hd, (h + 1) * hd)
            s = jnp.dot(k_ref[pl.ds(r0, tq), hs], qT_ref[hs, :], preferred_element_type=F32) + mask
            if with_bias:
                s = s + bias_ref[h, kb - (qb - 1)]
            s_ref[h] = s
            maxes.append(jnp.max(s, axis=0, keepdims=True))
        for h in range(N_HEADS):
            hs = slice(h * hd, (h + 1) * hd)
            m = m_ref[h]
            m_new = jnp.maximum(m, maxes[h])
            alpha = jnp.exp2(m - m_new)
            p = jnp.exp2(s_ref[h] - m_new)
            l_ref[h] = alpha * l_ref[h] + jnp.sum(p, axis=0, keepdims=True)
            acc_ref[h] = alpha * acc_ref[h] + jnp.dot(vT_ref[kb, hs, :], p.astype(BF16),
                                                      preferred_element_type=F32)
            m_ref[h] = m_new
        return carry

    near0 = jnp.maximum(qb - 1, 0)
    lax.fori_loop(0, near0, functools.partial(step, with_bias=False), 0)
    lax.fori_loop(near0, n_blk, functools.partial(step, with_bias=True), 0)
    for h in range(N_HEADS):
        o_ref[h * hd:(h + 1) * hd, :] = (acc_ref[h] / l_ref[h]).astype(o_ref.dtype)


def _layer_norm_t(r, g, b):
    mu = jnp.mean(r, axis=0, keepdims=True)
    c = r - mu
    var = jnp.mean(c * c, axis=0, keepdims=True)
    return c * lax.rsqrt(var + LN_EPS) * g + b


def _outproj_kernel(pool_ref, attnT_ref, x_ref, wpT_ref, waT_ref, g_ref, b_ref, hT_ref, hTb_ref,
                    *, alpha):
    mix = lax.dot_general(wpT_ref[...], pool_ref[...], (((1,), (1,)), ((), ())),
                          preferred_element_type=F32)
    mix = mix + jnp.dot(waT_ref[...], attnT_ref[...], preferred_element_type=F32)
    h = _layer_norm_t(alpha * x_ref[...].T + mix, g_ref[...], b_ref[...])
    hT_ref[...] = h
    hTb_ref[...] = h.astype(BF16)


def _out_projection(pool_out, attnT, x2, wpT, waT, g, b, alpha, tm):
    n, d = x2.shape
    d_pool = pool_out.shape[1]
    d_attn = attnT.shape[0]
    return pl.pallas_call(
        functools.partial(_outproj_kernel, alpha=alpha),
        grid=(n // tm,),
        in_specs=[pl.BlockSpec((tm, d_pool), lambda i: (i, 0)),
                  pl.BlockSpec((d_attn, tm), lambda i: (0, i)),
                  pl.BlockSpec((tm, d), lambda i: (i, 0)),
                  _resident((d, d_pool), lambda i: (0, 0)),
                  _resident((d, d_attn), lambda i: (0, 0)),
                  _resident((d, 1), lambda i: (0, 0)),
                  _resident((d, 1), lambda i: (0, 0))],
        out_specs=[pl.BlockSpec((d, tm), lambda i: (0, i)),
                   pl.BlockSpec((d, tm), lambda i: (0, i))],
        out_shape=[jax.ShapeDtypeStruct((d, n), F32), jax.ShapeDtypeStruct((d, n), BF16)],
        compiler_params=_params(1),
    )(pool_out, attnT, x2, wpT, waT, g, b)


def _peer_query_kernel(hTb_ref, wqT_ref, sk_ref, sT_ref):
    qhT = jnp.dot(wqT_ref[...], hTb_ref[...], preferred_element_type=F32).astype(BF16)
    half = sk_ref.shape[2]
    for c in range(sT_ref.shape[0]):
        sT_ref[c] = jnp.dot(sk_ref[c % 2], qhT[c * half:(c + 1) * half, :],
                            preferred_element_type=F32)


def _peer_query(hTb, wqT, sub_keys, tm):
    d, n = hTb.shape
    dq = wqT.shape[0]
    n_keys, half = sub_keys.shape[1:]
    n_slices = dq // half
    return pl.pallas_call(
        _peer_query_kernel,
        grid=(n // tm,),
        in_specs=[pl.BlockSpec((d, tm), lambda i: (0, i)),
                  _resident((dq, d), lambda i: (0, 0)),
                  _resident(sub_keys.shape, lambda i: (0, 0, 0))],
        out_specs=pl.BlockSpec((n_slices, n_keys, tm), lambda i: (0, 0, i)),
        out_shape=jax.ShapeDtypeStruct((n_slices, n_keys, n), F32),
        compiler_params=_params(1),
    )(hTb, wqT, sub_keys)


def _bitonic_sort_desc(xs):
    xs = list(xs)
    n = len(xs)
    k = 2
    while k <= n:
        j = k // 2
        while j >= 1:
            for i in range(n):
                o = i ^ j
                if o > i:
                    hi, lo = jnp.maximum(xs[i], xs[o]), jnp.minimum(xs[i], xs[o])
                    xs[i], xs[o] = (hi, lo) if (i & k) == 0 else (lo, hi)
            j //= 2
        k *= 2
    return xs


def _bitonic_merge_desc(xs):
    xs = list(xs)
    n = len(xs)
    j = n // 2
    while j >= 1:
        for i in range(n):
            o = i ^ j
            if o > i:
                xs[i], xs[o] = jnp.maximum(xs[i], xs[o]), jnp.minimum(xs[i], xs[o])
        j //= 2
    return xs


def _top_of_two_sorted(a, b):
    n = len(a)
    return [jnp.maximum(a[i], b[n - 1 - i]) for i in range(n)]


def _top16_over_keys(s):
    groups = s.shape[0] // SUBLANES
    xs = _bitonic_sort_desc([s[g * SUBLANES:(g + 1) * SUBLANES, :] for g in range(groups)])
    shift = SUBLANES // 2
    while shift >= 1:
        other = [pltpu.roll(a, shift, 0) for a in xs]
        xs = _bitonic_merge_desc(_top_of_two_sorted(xs, other))
        shift //= 2
    return xs


def _peer_route_kernel(sT_ref, c1_ref, e1_ref, r2_ref, e2_ref):
    k = PEER_TOPK
    assert PEER_HEADS == SUBLANES, "one head per sublane in the candidate stage"
    sub = lax.broadcasted_iota(I32, (SUBLANES, sT_ref.shape[2]), 0)
    a, b = None, None
    for h in range(PEER_HEADS):
        ah = _top16_over_keys(sT_ref[2 * h])
        bh = _top16_over_keys(sT_ref[2 * h + 1])
        a = ah if a is None else [jnp.where(sub == h, x, y) for x, y in zip(ah, a)]
        b = bh if b is None else [jnp.where(sub == h, x, y) for x, y in zip(bh, b)]
    pairs = [(i, j) for i in range(k) for j in range(k) if (i + 1) * (j + 1) <= k]
    cands = [a[i] + b[j] for i, j in pairs]
    pad = jnp.full_like(cands[0], -jnp.inf)
    padded = cands + [pad] * (-len(cands) % k)
    top = _bitonic_sort_desc(padded[:k])
    for g in range(1, len(padded) // k):
        nxt = _bitonic_sort_desc(padded[g * k:(g + 1) * k])
        top = _top_of_two_sorted(top, nxt)
        if g + 1 < len(padded) // k:
            top = _bitonic_merge_desc(top)
    cmax = a[0] + b[0]
    thr = functools.reduce(jnp.minimum, top)
    z = functools.reduce(jnp.add, [jnp.exp(t - cmax) for t in top])
    log2z = jnp.log(z) * LOG2E
    cut = []
    for r in range(k):
        hits = [jnp.where(c >= thr, 1.0, 0.0) for c, (i, j) in zip(cands, pairs) if i == r]
        cut.append(functools.reduce(jnp.add, hits) - 1.0)
    for h in range(PEER_HEADS):
        row = slice(h, h + 1)
        s1 = sT_ref[2 * h]
        s2 = sT_ref[2 * h + 1]
        c1 = jnp.full(s1.shape, -1.0, F32)
        r2 = jnp.full(s2.shape, float(k), F32)
        for r in reversed(range(k)):
            c1 = jnp.where(s1 == a[r][row, :], cut[r][row, :], c1)
            r2 = jnp.where(s2 == b[r][row, :], float(r), r2)
        c1_ref[h] = c1
        r2_ref[h] = r2.astype(BF16)
        e1_ref[h] = jnp.exp2((s1 - a[0][row, :]) * LOG2E)
        e2_ref[h] = jnp.exp2((s2 - b[0][row, :]) * LOG2E - log2z[row, :]).astype(BF16)


def _peer_route(sT, tt):
    n_slices, n_keys, n = sT.shape
    big = pl.BlockSpec((PEER_HEADS, n_keys, tt), lambda i: (0, 0, i))
    shape = (PEER_HEADS, n_keys, n)
    return pl.pallas_call(
        _peer_route_kernel,
        grid=(n // tt,),
        in_specs=[pl.BlockSpec((n_slices, n_keys, tt), lambda i: (0, 0, i))],
        out_specs=[big, big, big, big],
        out_shape=[jax.ShapeDtypeStruct(shape, F32), jax.ShapeDtypeStruct(shape, F32),
                   jax.ShapeDtypeStruct(shape, BF16), jax.ShapeDtypeStruct(shape, BF16)],
        compiler_params=_params(1),
    )(sT)


def _peer_dense_kernel(hTb_ref, u_ref, vT_ref, c1_ref, e1_ref, r2_ref, e2_ref, o_ref, *, chunk):
    ei = pl.program_id(1)
    te = u_ref.shape[0]
    rows = te // PEER_KEYS
    rows_per_chunk = chunk // PEER_KEYS

    @pl.when(ei == 0)
    def _():
        o_ref[...] = jnp.zeros_like(o_ref)

    i0 = pl.multiple_of(ei * rows, rows)

    def pre_act(c):
        return jnp.dot(u_ref[c * chunk:(c + 1) * chunk, :], hTb_ref[...],
                       preferred_element_type=F32)

    n_chunks = te // chunk
    act_next = pre_act(0)
    for c in range(n_chunks):
        act = act_next
        if c + 1 < n_chunks:
            act_next = pre_act(c + 1)
        gates = []
        for rr in range(rows_per_chunk):
            r = c * rows_per_chunk + rr
            gate = jnp.zeros((PEER_KEYS, act.shape[1]), BF16)
            for h in range(PEER_HEADS):
                cut = c1_ref[h, pl.ds(i0, rows), :][r:r + 1, :].astype(BF16)
                e1 = e1_ref[h, pl.ds(i0, rows), :][r:r + 1, :].astype(BF16)
                gate = gate + jnp.where(r2_ref[h] <= cut, e2_ref[h] * e1, jnp.zeros_like(gate))
            gates.append(gate)
        act = 0.5 * act * (1.0 + lax.erf(act * math.sqrt(0.5)))
        p = jnp.concatenate(
            [gates[rr] * act[rr * PEER_KEYS:(rr + 1) * PEER_KEYS, :].astype(BF16)
             for rr in range(rows_per_chunk)], axis=0)
        o_ref[...] += jnp.dot(vT_ref[:, c * chunk:(c + 1) * chunk], p, preferred_element_type=F32)


def _peer_dense(hTb, u, vT, c1, e1, r2, e2, tt, te, chunk):
    d, n = hTb.shape
    n_exp = u.shape[0]
    n_keys = c1.shape[1]
    big = _resident((PEER_HEADS, n_keys, tt), lambda t, e: (0, 0, t))
    return pl.pallas_call(
        functools.partial(_peer_dense_kernel, chunk=chunk),
        grid=(n // tt, n_exp // te),
        in_specs=[_resident((d, tt), lambda t, e: (0, t)),
                  pl.BlockSpec((te, d), lambda t, e: (e, 0)),
                  pl.BlockSpec((d, te), lambda t, e: (0, e)),
                  big, big, big, big],
        out_specs=pl.BlockSpec((d, tt), lambda t, e: (0, t)),
        out_shape=jax.ShapeDtypeStruct((d, n), F32),
        compiler_params=_params(2),
    )(hTb, u, vT, c1, e1, r2, e2)


def _final_ln_kernel(hT_ref, fT_ref, g_ref, b_ref, o_ref, *, alpha):
    y = _layer_norm_t(alpha * hT_ref[...] + fT_ref[...], g_ref[...], b_ref[...])
    o_ref[...] = y.T


def _final_ln(hT, ffnT, g, b, alpha, tm):
    d, n = hT.shape
    return pl.pallas_call(
        functools.partial(_final_ln_kernel, alpha=alpha),
        grid=(n // tm,),
        in_specs=[pl.BlockSpec((d, tm), lambda i: (0, i)),
                  pl.BlockSpec((d, tm), lambda i: (0, i)),
                  _resident((d, 1), lambda i: (0, 0)),
                  _resident((d, 1), lambda i: (0, 0))],
        out_specs=pl.BlockSpec((tm, d), lambda i: (i, 0)),
        out_shape=jax.ShapeDtypeStruct((n, d), F32),
        compiler_params=_params(1),
    )(hT, ffnT, g, b)


def _tile(n, pref):
    t = min(pref, n)
    while n % t:
        t -= LANES
    return t


def _layer(h2, batch, seq, w_in, pool_w, pool_scale, bias, w_out, ln1_g, ln1_b,
           peer_wq, sub_keys, peer_u, peer_v, ln2_g, ln2_b, alpha, tq):
    n, d = h2.shape
    d_pool = pool_w.shape[0] * pool_w.shape[1]
    d_attn = w_out.shape[0] - d_pool
    hd = d_attn // N_HEADS
    d_qi = IDX_HEADS * IDX_DIM
    top_k = min(MAX_TOPK, seq // 4)

    o = 0
    w_pool = w_in[:, o:o + d_pool]; o += d_pool
    w_q = w_in[:, o:o + d_attn] * (hd ** -0.5 * LOG2E); o += d_attn
    w_k = w_in[:, o:o + d_attn]; o += d_attn
    w_v = w_in[:, o:o + d_attn]; o += d_attn
    w_qi = w_in[:, o:o + d_qi]; o += d_qi
    w_ki = w_in[:, o:o + IDX_DIM]; o += IDX_DIM
    w_wi = w_in[:, o:o + IDX_HEADS] * (IDX_HEADS * IDX_DIM) ** -0.5
    zeros = jnp.zeros_like(w_ki)
    w_nat = jnp.concatenate([w_k, w_ki, zeros, zeros, w_ki], axis=1).astype(BF16)
    w_qT = jnp.concatenate([w_q.T, w_qi.T], axis=0).astype(BF16)

    p_pool, k_nat, qT_all, vT_blk, wiT = _projections(
        h2, w_pool.astype(BF16), w_nat, w_qT, w_v.T.astype(BF16), w_wi.T.astype(BF16),
        _tile(seq, 512), tq)

    pool_out = _pool_mixer(p_pool, pool_w.astype(BF16), pool_scale.reshape(1, d_pool), seq,
                           _tile(seq, 256))

    attnT = _sparse_attention_split(qT_all, wiT, k_nat, vT_blk, bias, batch, seq, tq, top_k, d_attn)

    w_pT = w_out[:d_pool].T.astype(BF16)
    w_aT = w_out[d_pool:].T.astype(BF16)
    hT, hTb = _out_projection(pool_out, attnT, h2, w_pT, w_aT, ln1_g.reshape(d, 1),
                              ln1_b.reshape(d, 1), alpha, _tile(n, 256))

    sT = _peer_query(hTb, peer_wq.T.astype(BF16), sub_keys.astype(BF16), _tile(n, 512))
    c1, e1, r2, e2 = _peer_route(sT, _tile(n, 256))
    ffnT = _peer_dense(hTb, peer_u.astype(BF16), peer_v.T.astype(BF16), c1, e1, r2, e2,
                       _tile(n, 512), 16 * PEER_KEYS, 4 * PEER_KEYS)
    return _final_ln(hT, ffnT, ln2_g.reshape(d, 1), ln2_b.reshape(d, 1), alpha, _tile(n, 256))


def _sparse_attention_split(qT_all, wiT, k_nat, vT_blk, bias, batch, seq, tq, top_k, d_attn):
    n = batch * seq
    nq = seq // tq
    hd = d_attn // N_HEADS
    d_qi = IDX_HEADS * IDX_DIM
    assert d_qi == d_attn, "q and indexer-q row blocks must have equal height"
    tok0 = lambda b, q: (0, b * nq + q)
    tok1 = lambda b, q: (1, b * nq + q)
    return pl.pallas_call(
        functools.partial(_attn_kernel, top_k=top_k, tq=tq, hd=hd),
        grid=(batch, nq),
        in_specs=[pl.BlockSpec((d_attn, tq), tok0),
                  pl.BlockSpec((d_qi, tq), tok1),
                  pl.BlockSpec((IDX_HEADS, tq), tok0),
                  pl.BlockSpec((seq, d_attn), lambda b, q: (b, 0)),
                  pl.BlockSpec((seq, 4 * IDX_DIM), lambda b, q: (b, d_attn // (4 * IDX_DIM))),
                  pl.BlockSpec((nq, d_attn, tq), lambda b, q: (b, 0, 0)),
                  _resident(bias.shape, lambda b, q: (0, 0, 0, 0))],
        out_specs=pl.BlockSpec((d_attn, tq), tok0),
        out_shape=jax.ShapeDtypeStruct((d_attn, n), BF16),
        scratch_shapes=[pltpu.VMEM((seq, tq), F32),
                        pltpu.VMEM((N_HEADS, 1, tq), F32),
                        pltpu.VMEM((N_HEADS, 1, tq), F32),
                        pltpu.VMEM((N_HEADS, hd, tq), F32),
                        pltpu.VMEM((N_HEADS, tq, tq), F32)],
        compiler_params=_params(2),
    )(qT_all, qT_all, wiT, k_nat, k_nat, vT_blk, bias)


def kernel(x, w_in, pool_w, pool_scale, rel_bias, w_out, ln1_g, ln1_b, peer_wq, peer_subkeys,
           peer_u, peer_v, ln2_g, ln2_b):
    batch, seq, d = x.shape
    depth = w_in.shape[0]
    alpha = (2 * depth) ** 0.25
    tq = min(256, seq)
    bias = _bias_tiles(rel_bias, tq)
    h2 = x.reshape(batch * seq, d)
    for l in range(depth):
        h2 = _layer(h2, batch, seq, w_in[l], pool_w[l], pool_scale[l], bias, w_out[l],
                    ln1_g[l], ln1_b[l], peer_wq[l], peer_subkeys[l], peer_u[l], peer_v[l],
                    ln2_g[l], ln2_b[l], alpha, tq)
    return h2.reshape(batch, seq, d)
```

```python
import functools
import math

import jax
import jax.numpy as jnp
from jax import lax
from jax.experimental import pallas as pl
from jax.experimental.pallas import tpu as pltpu

F32 = jnp.float32
BF16 = jnp.bfloat16
I32 = jnp.int32

LANES = 128
SUBLANES = 8
VMEM_LIMIT = 56 * 1024 * 1024

POOL_WINDOWS = (2, 4, 8, 16)
N_HEADS = 8
IDX_HEADS = 16
IDX_DIM = 64
MAX_TOPK = 256
N_BUCKETS = 32
MAX_DISTANCE = 128
PEER_HEADS = 8
PEER_KEYS = 128
PEER_TOPK = 16
LN_EPS = 1e-5
NEG_INF = -1e30
M_FLOOR = -1e20
BISECT_UNCHECKED = 16
BISECT_CAP = 48
LOG2E = 1.0 / math.log(2.0)


def _params(n_grid):
    return pltpu.CompilerParams(dimension_semantics=("arbitrary",) * n_grid,
                                vmem_limit_bytes=VMEM_LIMIT)


def _resident(shape, index_map):
    return pl.BlockSpec(shape, index_map, pipeline_mode=pl.Buffered(1))


def _proj_kernel(x_ref, wp_ref, wn_ref, wq_ref, wv_ref, ww_ref, pool_ref, nat_ref, qT_ref, vT_ref,
                 wiT_ref):
    x = x_ref[...].astype(BF16)
    nt = (((1,), (1,)), ((), ()))
    pool_ref[...] = jnp.dot(x, wp_ref[...], preferred_element_type=F32)
    nat_ref[...] = jnp.dot(x, wn_ref[...], preferred_element_type=F32).astype(nat_ref.dtype)
    qT_ref[...] = lax.dot_general(wq_ref[...], x, nt, preferred_element_type=F32).astype(qT_ref.dtype)
    vT = lax.dot_general(wv_ref[...], x, nt, preferred_element_type=F32).astype(vT_ref.dtype)
    slab = vT_ref.shape[2]
    for j in range(vT_ref.shape[0]):
        vT_ref[j] = vT[:, j * slab:(j + 1) * slab]
    wiT_ref[...] = lax.dot_general(ww_ref[...], x, nt, preferred_element_type=F32)


def _projections(x2, w_pool, w_nat, w_qT, w_vT, w_wiT, tm, slab):
    n, d = x2.shape
    full = lambda w: _resident(w.shape, lambda i: (0, 0))
    return pl.pallas_call(
        _proj_kernel,
        grid=(n // tm,),
        in_specs=[pl.BlockSpec((tm, d), lambda i: (i, 0)),
                  full(w_pool), full(w_nat), full(w_qT), full(w_vT), full(w_wiT)],
        out_specs=[pl.BlockSpec((tm, w_pool.shape[1]), lambda i: (i, 0)),
                   pl.BlockSpec((tm, w_nat.shape[1]), lambda i: (i, 0)),
                   pl.BlockSpec((w_qT.shape[0], tm), lambda i: (0, i)),
                   pl.BlockSpec((tm // slab, w_vT.shape[0], slab), lambda i: (i, 0, 0)),
                   pl.BlockSpec((w_wiT.shape[0], tm), lambda i: (0, i))],
        out_shape=[jax.ShapeDtypeStruct((n, w_pool.shape[1]), F32),
                   jax.ShapeDtypeStruct((n, w_nat.shape[1]), BF16),
                   jax.ShapeDtypeStruct((w_qT.shape[0], n), BF16),
                   jax.ShapeDtypeStruct((n // slab, w_vT.shape[0], slab), BF16),
                   jax.ShapeDtypeStruct((w_wiT.shape[0], n), F32)],
        compiler_params=_params(1),
    )(x2, w_pool, w_nat, w_qT, w_vT, w_wiT)


def _pool_kernel(v_ref, halo_ref, w_ref, sc_ref, o_ref, buf_ref, *, seq_tiles, halo):
    ts = v_ref.shape[0]
    group = w_ref.shape[1]
    tile = pl.program_id(0) % seq_tiles
    buf_ref[0:halo, :] = jnp.where(tile == 0, 0.0, halo_ref[...])
    buf_ref[halo:, :] = v_ref[...]
    pos = tile * ts + lax.broadcasted_iota(I32, (ts, 1), 0)
    for g, win in enumerate(POOL_WINDOWS):
        cols = slice(g * group, (g + 1) * group)
        tok = buf_ref[halo:halo + ts, cols]
        acc = tok
        for j in range(1, win):
            acc = acc + buf_ref[halo - j:halo - j + ts, cols]
        cnt = jnp.minimum(pos + 1, win).astype(F32)
        pooled = acc / cnt - tok
        mixed = jnp.dot(pooled.astype(BF16), w_ref[g], preferred_element_type=F32)
        o_ref[:, cols] = (mixed * sc_ref[:, cols]).astype(o_ref.dtype)


def _pool_mixer(p_pool, pool_w, pool_scale, seq, ts):
    n, d_pool = p_pool.shape
    halo = max(POOL_WINDOWS)
    seq_tiles = seq // ts
    return pl.pallas_call(
        functools.partial(_pool_kernel, seq_tiles=seq_tiles, halo=halo),
        grid=(n // ts,),
        in_specs=[pl.BlockSpec((ts, d_pool), lambda r: (r, 0)),
                  pl.BlockSpec((halo, d_pool), lambda r: (jnp.maximum(r * (ts // halo) - 1, 0), 0)),
                  pl.BlockSpec(pool_w.shape, lambda r: (0, 0, 0)),
                  pl.BlockSpec((1, d_pool), lambda r: (0, 0))],
        out_specs=pl.BlockSpec((ts, d_pool), lambda r: (r, 0)),
        out_shape=jax.ShapeDtypeStruct((n, d_pool), BF16),
        scratch_shapes=[pltpu.VMEM((halo + ts, d_pool), F32)],
        compiler_params=_params(1),
    )(p_pool, p_pool, pool_w, pool_scale)


def _bias_kernel(rb_ref, o_ref, *, tq):
    n_heads, n_slabs, ks, _ = o_ref.shape
    max_exact = N_BUCKETS // 2
    for j in range(n_slabs):
        dist = (lax.broadcasted_iota(I32, (ks, tq), 1) + (tq - ks * j)
                - lax.broadcasted_iota(I32, (ks, tq), 0))
        dist = jnp.maximum(dist, 0)
        d = jnp.maximum(dist, 1).astype(F32)
        large = max_exact + (jnp.log(d / max_exact) / math.log(MAX_DISTANCE / max_exact)
                             * (N_BUCKETS - max_exact))
        coord = jnp.where(dist < max_exact, dist.astype(F32), large)
        for h in range(n_heads):
            def pick(b, acc, h=h, coord=coord):
                return jnp.where(coord >= jnp.asarray(b, F32), rb_ref[b, h], acc)
            bias = lax.fori_loop(1, N_BUCKETS, pick, jnp.full((ks, tq), rb_ref[0, h], F32))
            o_ref[h, j] = (bias - rb_ref[N_BUCKETS - 1, h]) * LOG2E


def _bias_tiles(rel_bias, tq):
    n_heads = rel_bias.shape[1]
    return pl.pallas_call(
        functools.partial(_bias_kernel, tq=tq),
        in_specs=[pl.BlockSpec(memory_space=pltpu.SMEM)],
        out_specs=pl.BlockSpec(memory_space=pltpu.VMEM),
        out_shape=jax.ShapeDtypeStruct((n_heads, 2, tq, tq), F32),
        compiler_params=pltpu.CompilerParams(vmem_limit_bytes=VMEM_LIMIT),
    )(rel_bias)


def _attn_kernel(qT_ref, qiT_ref, wiT_ref, k_ref, ki2_ref, vT_ref, bias_ref, o_ref,
                 key_ref, m_ref, l_ref, acc_ref, s_ref, *, top_k, tq, hd):
    ks = LANES
    halves = tq // ks
    qb = pl.program_id(1)
    n_blk = qb + 1
    q_pos = qb * tq + lax.broadcasted_iota(I32, (1, tq), 1)

    def key_pos(r0):
        return r0 + lax.broadcasted_iota(I32, (ks, 1), 0)

    w_all = wiT_ref[...]

    def score_blk(kb, carry):
        lo, hi = carry
        for hf in range(halves):
            r0 = pl.multiple_of(kb * tq + hf * ks, ks)
            acc = jnp.zeros((ks, tq), F32)
            for p in range(IDX_HEADS // 2):
                qp = qiT_ref[p * LANES:(p + 1) * LANES, :]
                z0 = jnp.dot(ki2_ref[pl.ds(r0, ks), 0:LANES], qp, preferred_element_type=F32)
                z1 = jnp.dot(ki2_ref[pl.ds(r0, ks), LANES:2 * LANES], qp,
                             preferred_element_type=F32)
                acc = acc + w_all[2 * p:2 * p + 1, :] * jnp.maximum(z0, 0.0)
                acc = acc + w_all[2 * p + 1:2 * p + 2, :] * jnp.maximum(z1, 0.0)
            causal = key_pos(r0) <= q_pos
            key_ref[pl.ds(r0, ks), :] = jnp.where(causal, acc, NEG_INF)
            hi = jnp.maximum(hi, jnp.max(jnp.where(causal, acc, -jnp.inf), axis=0, keepdims=True))
            lo = jnp.minimum(lo, jnp.min(jnp.where(causal, acc, jnp.inf), axis=0, keepdims=True))
        return lo, hi

    lo, hi = lax.fori_loop(0, n_blk, score_blk,
                           (jnp.full((1, tq), jnp.inf, F32), jnp.full((1, tq), -jnp.inf, F32)))

    def pending(state):
        i, _, _, cnt_lo = state
        return jnp.logical_and(i < BISECT_CAP, jnp.max(cnt_lo) > top_k)

    def bisect(state):
        i, lo, hi, cnt_lo = state
        mid = lo + 0.5 * (hi - lo)

        def count_blk(kb, c8):
            for hf in range(halves):
                r0 = pl.multiple_of(kb * tq + hf * ks, ks)
                hit = (key_ref[pl.ds(r0, ks), :] >= mid).astype(F32)
                c8 = c8 + jnp.sum(hit.reshape(ks // SUBLANES, SUBLANES, tq), axis=0)
            return c8

        c8 = lax.fori_loop(0, n_blk, count_blk, jnp.zeros((SUBLANES, tq), F32))
        cnt = jnp.sum(c8, axis=0, keepdims=True)
        take = cnt >= top_k
        return (i + 1, jnp.where(take, mid, lo), jnp.where(take, hi, mid),
                jnp.where(take, cnt, cnt_lo))

    state = (jnp.int32(0), lo, hi, (q_pos + 1).astype(F32))
    state = lax.fori_loop(0, BISECT_UNCHECKED, lambda _, s: bisect(s), state)
    _, thr, _, _ = lax.while_loop(pending, bisect, state)

    def mask_blk(kb, carry):
        for hf in range(halves):
            r0 = pl.multiple_of(kb * tq + hf * ks, ks)
            sel = (key_ref[pl.ds(r0, ks), :] >= thr) & (key_pos(r0) <= q_pos)
            key_ref[pl.ds(r0, ks), :] = jnp.where(sel, 0.0, NEG_INF).astype(F32)
        return carry

    lax.fori_loop(0, n_blk, mask_blk, 0)

    m_ref[...] = jnp.full(m_ref.shape, M_FLOOR, F32)
    l_ref[...] = jnp.zeros(l_ref.shape, F32)
    acc_ref[...] = jnp.zeros(acc_ref.shape, F32)

    def step(kb, carry, with_bias):
        r0 = pl.multiple_of(kb * tq, tq)
        mask = key_ref[pl.ds(r0, tq), :]
        maxes = []
        for h in range(N_HEADS):
            hs = slice(h * hd, (h + 1) * hd)
            s = jnp.dot(k_ref[pl.ds(r0, tq), hs], qT_ref[hs, :], preferred_element_type=F32) + mask
            if with_bias:
                s = s + bias_ref[h, kb - (qb - 1)]
            s_ref[h] = s
            maxes.append(jnp.max(s, axis=0, keepdims=True))
        for h in range(N_HEADS):
            hs = slice(h * hd, (h + 1) * hd)
            m = m_ref[h]
            m_new = jnp.maximum(m, maxes[h])
            alpha = jnp.exp2(m - m_new)
            p = jnp.exp2(s_ref[h] - m_new)
            l_ref[h] = alpha * l_ref[h] + jnp.sum(p, axis=0, keepdims=True)
            acc_ref[h] = alpha * acc_ref[h] + jnp.dot(vT_ref[kb, hs, :], p.astype(BF16),
                                                      preferred_element_type=F32)
            m_ref[h] = m_new
        return carry

    near0 = jnp.maximum(qb - 1, 0)
    lax.fori_loop(0, near0, functools.partial(step, with_bias=False), 0)
    lax.fori_loop(near0, n_blk, functools.partial(step, with_bias=True), 0)
    for h in range(N_HEADS):
        o_ref[h * hd:(h + 1) * hd, :] = (acc_ref[h] / l_ref[h]).astype(o_ref.dtype)


def _layer_norm_t(r, g, b):
    mu = jnp.mean(r, axis=0, keepdims=True)
    c = r - mu
    var = jnp.mean(c * c, axis=0, keepdims=True)
    return c * lax.rsqrt(var + LN_EPS) * g + b


def _outproj_kernel(pool_ref, attnT_ref, x_ref, wpT_ref, waT_ref, g_ref, b_ref, hT_ref, hTb_ref,
                    *, alpha):
    mix = lax.dot_general(wpT_ref[...], pool_ref[...], (((1,), (1,)), ((), ())),
                          preferred_element_type=F32)
    mix = mix + jnp.dot(waT_ref[...], attnT_ref[...], preferred_element_type=F32)
    h = _layer_norm_t(alpha * x_ref[...].T + mix, g_ref[...], b_ref[...])
    hT_ref[...] = h
    hTb_ref[...] = h.astype(BF16)


def _out_projection(pool_out, attnT, x2, wpT, waT, g, b, alpha, tm):
    n, d = x2.shape
    d_pool = pool_out.shape[1]
    d_attn = attnT.shape[0]
    return pl.pallas_call(
        functools.partial(_outproj_kernel, alpha=alpha),
        grid=(n // tm,),
        in_specs=[pl.BlockSpec((tm, d_pool), lambda i: (i, 0)),
                  pl.BlockSpec((d_attn, tm), lambda i: (0, i)),
                  pl.BlockSpec((tm, d), lambda i: (i, 0)),
                  _resident((d, d_pool), lambda i: (0, 0)),
                  _resident((d, d_attn), lambda i: (0, 0)),
                  _resident((d, 1), lambda i: (0, 0)),
                  _resident((d, 1), lambda i: (0, 0))],
        out_specs=[pl.BlockSpec((d, tm), lambda i: (0, i)),
                   pl.BlockSpec((d, tm), lambda i: (0, i))],
        out_shape=[jax.ShapeDtypeStruct((d, n), F32), jax.ShapeDtypeStruct((d, n), BF16)],
        compiler_params=_params(1),
    )(pool_out, attnT, x2, wpT, waT, g, b)


def _peer_query_kernel(hTb_ref, wqT_ref, sk_ref, sT_ref):
    qhT = jnp.dot(wqT_ref[...], hTb_ref[...], preferred_element_type=F32).astype(BF16)
    half = sk_ref.shape[2]
    for c in range(sT_ref.shape[0]):
        sT_ref[c] = jnp.dot(sk_ref[c % 2], qhT[c * half:(c + 1) * half, :],
                            preferred_element_type=F32)


def _peer_query(hTb, wqT, sub_keys, tm):
    d, n = hTb.shape
    dq = wqT.shape[0]
    n_keys, half = sub_keys.shape[1:]
    n_slices = dq // half
    return pl.pallas_call(
        _peer_query_kernel,
        grid=(n // tm,),
        in_specs=[pl.BlockSpec((d, tm), lambda i: (0, i)),
                  _resident((dq, d), lambda i: (0, 0)),
                  _resident(sub_keys.shape, lambda i: (0, 0, 0))],
        out_specs=pl.BlockSpec((n_slices, n_keys, tm), lambda i: (0, 0, i)),
        out_shape=jax.ShapeDtypeStruct((n_slices, n_keys, n), F32),
        compiler_params=_params(1),
    )(hTb, wqT, sub_keys)


def _bitonic_sort_desc(xs):
    xs = list(xs)
    n = len(xs)
    k = 2
    while k <= n:
        j = k // 2
        while j >= 1:
            for i in range(n):
                o = i ^ j
                if o > i:
                    hi, lo = jnp.maximum(xs[i], xs[o]), jnp.minimum(xs[i], xs[o])
                    xs[i], xs[o] = (hi, lo) if (i & k) == 0 else (lo, hi)
            j //= 2
        k *= 2
    return xs


def _bitonic_merge_desc(xs):
    xs = list(xs)
    n = len(xs)
    j = n // 2
    while j >= 1:
        for i in range(n):
            o = i ^ j
            if o > i:
                xs[i], xs[o] = jnp.maximum(xs[i], xs[o]), jnp.minimum(xs[i], xs[o])
        j //= 2
    return xs


def _top_of_two_sorted(a, b):
    n = len(a)
    return [jnp.maximum(a[i], b[n - 1 - i]) for i in range(n)]


def _top16_over_keys(s):
    groups = s.shape[0] // SUBLANES
    xs = _bitonic_sort_desc([s[g * SUBLANES:(g + 1) * SUBLANES, :] for g in range(groups)])
    shift = SUBLANES // 2
    while shift >= 1:
        other = [pltpu.roll(a, shift, 0) for a in xs]
        xs = _bitonic_merge_desc(_top_of_two_sorted(xs, other))
        shift //= 2
    return xs


def _peer_route_kernel(sT_ref, c1_ref, e1_ref, r2_ref, e2_ref):
    k = PEER_TOPK
    assert PEER_HEADS == SUBLANES, "one head per sublane in the candidate stage"
    sub = lax.broadcasted_iota(I32, (SUBLANES, sT_ref.shape[2]), 0)
    a, b = None, None
    for h in range(PEER_HEADS):
        ah = _top16_over_keys(sT_ref[2 * h])
        bh = _top16_over_keys(sT_ref[2 * h + 1])
        a = ah if a is None else [jnp.where(sub == h, x, y) for x, y in zip(ah, a)]
        b = bh if b is None else [jnp.where(sub == h, x, y) for x, y in zip(bh, b)]
    pairs = [(i, j) for i in range(k) for j in range(k) if (i + 1) * (j + 1) <= k]
    cands = [a[i] + b[j] for i, j in pairs]
    pad = jnp.full_like(cands[0], -jnp.inf)
    padded = cands + [pad] * (-len(cands) % k)
    top = _bitonic_sort_desc(padded[:k])
    for g in range(1, len(padded) // k):
        nxt = _bitonic_sort_desc(padded[g * k:(g + 1) * k])
        top = _top_of_two_sorted(top, nxt)
        if g + 1 < len(padded) // k:
            top = _bitonic_merge_desc(top)
    cmax = a[0] + b[0]
    thr = functools.reduce(jnp.minimum, top)
    z = functools.reduce(jnp.add, [jnp.exp(t - cmax) for t in top])
    log2z = jnp.log(z) * LOG2E
    cut = []
    for r in range(k):
        hits = [jnp.where(c >= thr, 1.0, 0.0) for c, (i, j) in zip(cands, pairs) if i == r]
        cut.append(functools.reduce(jnp.add, hits) - 1.0)
    for h in range(PEER_HEADS):
        row = slice(h, h + 1)
        s1 = sT_ref[2 * h]
        s2 = sT_ref[2 * h + 1]
        c1 = jnp.full(s1.shape, -1.0, F32)
        r2 = jnp.full(s2.shape, float(k), F32)
        for r in reversed(range(k)):
            c1 = jnp.where(s1 == a[r][row, :], cut[r][row, :], c1)
            r2 = jnp.where(s2 == b[r][row, :], float(r), r2)
        c1_ref[h] = c1
        r2_ref[h] = r2.astype(BF16)
        e1_ref[h] = jnp.exp2((s1 - a[0][row, :]) * LOG2E)
        e2_ref[h] = jnp.exp2((s2 - b[0][row, :]) * LOG2E - log2z[row, :]).astype(BF16)


def _peer_route(sT, tt):
    n_slices, n_keys, n = sT.shape
    big = pl.BlockSpec((PEER_HEADS, n_keys, tt), lambda i: (0, 0, i))
    shape = (PEER_HEADS, n_keys, n)
    return pl.pallas_call(
        _peer_route_kernel,
        grid=(n // tt,),
        in_specs=[pl.BlockSpec((n_slices, n_keys, tt), lambda i: (0, 0, i))],
        out_specs=[big, big, big, big],
        out_shape=[jax.ShapeDtypeStruct(shape, F32), jax.ShapeDtypeStruct(shape, F32),
                   jax.ShapeDtypeStruct(shape, BF16), jax.ShapeDtypeStruct(shape, BF16)],
        compiler_params=_params(1),
    )(sT)


def _peer_dense_kernel(hTb_ref, u_ref, vT_ref, c1_ref, e1_ref, r2_ref, e2_ref, o_ref, *, chunk):
    ei = pl.program_id(1)
    te = u_ref.shape[0]
    rows = te // PEER_KEYS
    rows_per_chunk = chunk // PEER_KEYS

    @pl.when(ei == 0)
    def _():
        o_ref[...] = jnp.zeros_like(o_ref)

    i0 = pl.multiple_of(ei * rows, rows)

    def pre_act(c):
        return jnp.dot(u_ref[c * chunk:(c + 1) * chunk, :], hTb_ref[...],
                       preferred_element_type=F32)

    n_chunks = te // chunk
    act_next = pre_act(0)
    for c in range(n_chunks):
        act = act_next
        if c + 1 < n_chunks:
            act_next = pre_act(c + 1)
        gates = []
        for rr in range(rows_per_chunk):
            r = c * rows_per_chunk + rr
            gate = jnp.zeros((PEER_KEYS, act.shape[1]), BF16)
            for h in range(PEER_HEADS):
                cut = c1_ref[h, pl.ds(i0, rows), :][r:r + 1, :].astype(BF16)
                e1 = e1_ref[h, pl.ds(i0, rows), :][r:r + 1, :].astype(BF16)
                gate = gate + jnp.where(r2_ref[h] <= cut, e2_ref[h] * e1, jnp.zeros_like(gate))
            gates.append(gate)
        act = 0.5 * act * (1.0 + lax.erf(act * math.sqrt(0.5)))
        p = jnp.concatenate(
            [gates[rr] * act[rr * PEER_KEYS:(rr + 1) * PEER_KEYS, :].astype(BF16)
             for rr in range(rows_per_chunk)], axis=0)
        o_ref[...] += jnp.dot(vT_ref[:, c * chunk:(c + 1) * chunk], p, preferred_element_type=F32)


def _peer_dense(hTb, u, vT, c1, e1, r2, e2, tt, te, chunk):
    d, n = hTb.shape
    n_exp = u.shape[0]
    n_keys = c1.shape[1]
    big = pl.BlockSpec((PEER_HEADS, n_keys, tt), lambda t, e: (0, 0, t))
    return pl.pallas_call(
        functools.partial(_peer_dense_kernel, chunk=chunk),
        grid=(n // tt, n_exp // te),
        in_specs=[pl.BlockSpec((d, tt), lambda t, e: (0, t)),
                  pl.BlockSpec((te, d), lambda t, e: (e, 0)),
                  pl.BlockSpec((d, te), lambda t, e: (0, e)),
                  big, big, big, big],
        out_specs=pl.BlockSpec((d, tt), lambda t, e: (0, t)),
        out_shape=jax.ShapeDtypeStruct((d, n), F32),
        compiler_params=_params(2),
    )(hTb, u, vT, c1, e1, r2, e2)


def _final_ln_kernel(hT_ref, fT_ref, g_ref, b_ref, o_ref, *, alpha):
    y = _layer_norm_t(alpha * hT_ref[...] + fT_ref[...], g_ref[...], b_ref[...])
    o_ref[...] = y.T


def _final_ln(hT, ffnT, g, b, alpha, tm):
    d, n = hT.shape
    return pl.pallas_call(
        functools.partial(_final_ln_kernel, alpha=alpha),
        grid=(n // tm,),
        in_specs=[pl.BlockSpec((d, tm), lambda i: (0, i)),
                  pl.BlockSpec((d, tm), lambda i: (0, i)),
                  _resident((d, 1), lambda i: (0, 0)),
                  _resident((d, 1), lambda i: (0, 0))],
        out_specs=pl.BlockSpec((tm, d), lambda i: (i, 0)),
        out_shape=jax.ShapeDtypeStruct((n, d), F32),
        compiler_params=_params(1),
    )(hT, ffnT, g, b)


def _tile(n, pref):
    t = min(pref, n)
    while n % t:
        t -= LANES
    return t


def _layer(h2, batch, seq, w_in, pool_w, pool_scale, bias, w_out, ln1_g, ln1_b,
           peer_wq, sub_keys, peer_u, peer_v, ln2_g, ln2_b, alpha, tq):
    n, d = h2.shape
    d_pool = pool_w.shape[0] * pool_w.shape[1]
    d_attn = w_out.shape[0] - d_pool
    hd = d_attn // N_HEADS
    d_qi = IDX_HEADS * IDX_DIM
    top_k = min(MAX_TOPK, seq // 4)

    o = 0
    w_pool = w_in[:, o:o + d_pool]; o += d_pool
    w_q = w_in[:, o:o + d_attn] * (hd ** -0.5 * LOG2E); o += d_attn
    w_k = w_in[:, o:o + d_attn]; o += d_attn
    w_v = w_in[:, o:o + d_attn]; o += d_attn
    w_qi = w_in[:, o:o + d_qi]; o += d_qi
    w_ki = w_in[:, o:o + IDX_DIM]; o += IDX_DIM
    w_wi = w_in[:, o:o + IDX_HEADS] * (IDX_HEADS * IDX_DIM) ** -0.5
    zeros = jnp.zeros_like(w_ki)
    w_nat = jnp.concatenate([w_k, w_ki, zeros, zeros, w_ki], axis=1).astype(BF16)
    w_qT = jnp.concatenate([w_q.T, w_qi.T], axis=0).astype(BF16)

    p_pool, k_nat, qT_all, vT_blk, wiT = _projections(
        h2, w_pool.astype(BF16), w_nat, w_qT, w_v.T.astype(BF16), w_wi.T.astype(BF16),
        _tile(seq, 512), tq)

    pool_out = _pool_mixer(p_pool, pool_w.astype(BF16), pool_scale.reshape(1, d_pool), seq,
                           _tile(seq, 256))

    attnT = _sparse_attention_split(qT_all, wiT, k_nat, vT_blk, bias, batch, seq, tq, top_k, d_attn)

    w_pT = w_out[:d_pool].T.astype(BF16)
    w_aT = w_out[d_pool:].T.astype(BF16)
    hT, hTb = _out_projection(pool_out, attnT, h2, w_pT, w_aT, ln1_g.reshape(d, 1),
                              ln1_b.reshape(d, 1), alpha, _tile(n, 256))

    sT = _peer_query(hTb, peer_wq.T.astype(BF16), sub_keys.astype(BF16), _tile(n, 512))
    c1, e1, r2, e2 = _peer_route(sT, _tile(n, 256))
    ffnT = _peer_dense(hTb, peer_u.astype(BF16), peer_v.T.astype(BF16), c1, e1, r2, e2,
                       _tile(n, 512), 8 * PEER_KEYS, 4 * PEER_KEYS)
    return _final_ln(hT, ffnT, ln2_g.reshape(d, 1), ln2_b.reshape(d, 1), alpha, _tile(n, 256))


def _sparse_attention_split(qT_all, wiT, k_nat, vT_blk, bias, batch, seq, tq, top_k, d_attn):
    n = batch * seq
    nq = seq // tq
    hd = d_attn // N_HEADS
    d_qi = IDX_HEADS * IDX_DIM
    assert d_qi == d_attn, "q and indexer-q row blocks must have equal height"
    tok0 = lambda b, q: (0, b * nq + q)
    tok1 = lambda b, q: (1, b * nq + q)
    return pl.pallas_call(
        functools.partial(_attn_kernel, top_k=top_k, tq=tq, hd=hd),
        grid=(batch, nq),
        in_specs=[pl.BlockSpec((d_attn, tq), tok0),
                  pl.BlockSpec((d_qi, tq), tok1),
                  pl.BlockSpec((IDX_HEADS, tq), tok0),
                  pl.BlockSpec((seq, d_attn), lambda b, q: (b, 0)),
                  pl.BlockSpec((seq, 4 * IDX_DIM), lambda b, q: (b, d_attn // (4 * IDX_DIM))),
                  pl.BlockSpec((nq, d_attn, tq), lambda b, q: (b, 0, 0)),
                  _resident(bias.shape, lambda b, q: (0, 0, 0, 0))],
        out_specs=pl.BlockSpec((d_attn, tq), tok0),
        out_shape=jax.ShapeDtypeStruct((d_attn, n), BF16),
        scratch_shapes=[pltpu.VMEM((seq, tq), F32),
                        pltpu.VMEM((N_HEADS, 1, tq), F32),
                        pltpu.VMEM((N_HEADS, 1, tq), F32),
                        pltpu.VMEM((N_HEADS, hd, tq), F32),
                        pltpu.VMEM((N_HEADS, tq, tq), F32)],
        compiler_params=_params(2),
    )(qT_all, qT_all, wiT, k_nat, k_nat, vT_blk, bias)


def kernel(x, w_in, pool_w, pool_scale, rel_bias, w_out, ln1_g, ln1_b, peer_wq, peer_subkeys,
           peer_u, peer_v, ln2_g, ln2_b):
    batch, seq, d = x.shape
    depth = w_in.shape[0]
    alpha = (2 * depth) ** 0.25
    tq = min(256, seq)
    bias = _bias_tiles(rel_bias, tq)
    h2 = x.reshape(batch * seq, d)
    for l in range(depth):
        h2 = _layer(h2, batch, seq, w_in[l], pool_w[l], pool_scale[l], bias, w_out[l],
                    ln1_g[l], ln1_b[l], peer_wq[l], peer_subkeys[l], peer_u[l], peer_v[l],
                    ln2_g[l], ln2_b[l], alpha, tq)
    return h2.reshape(batch, seq, d)
```

```python
import functools
import math

import jax
import jax.numpy as jnp
from jax import lax
from jax.experimental import pallas as pl
from jax.experimental.pallas import tpu as pltpu

F32 = jnp.float32
BF16 = jnp.bfloat16
I32 = jnp.int32

LANES = 128
SUBLANES = 8
VMEM_LIMIT = 56 * 1024 * 1024

POOL_WINDOWS = (2, 4, 8, 16)
N_HEADS = 8
IDX_HEADS = 16
IDX_DIM = 64
MAX_TOPK = 256
N_BUCKETS = 32
MAX_DISTANCE = 128
PEER_HEADS = 8
PEER_KEYS = 128
PEER_TOPK = 16
LN_EPS = 1e-5
NEG_INF = -1e30
M_FLOOR = -1e20
BISECT_UNCHECKED = 16
BISECT_CAP = 48
LOG2E = 1.0 / math.log(2.0)


def _params(n_grid):
    return pltpu.CompilerParams(dimension_semantics=("arbitrary",) * n_grid,
                                vmem_limit_bytes=VMEM_LIMIT)


def _resident(shape, index_map):
    return pl.BlockSpec(shape, index_map, pipeline_mode=pl.Buffered(1))


def _proj_kernel(x_ref, wp_ref, wn_ref, wq_ref, wv_ref, ww_ref, pool_ref, nat_ref, qT_ref, vT_ref,
                 wiT_ref):
    x = x_ref[...].astype(BF16)
    nt = (((1,), (1,)), ((), ()))
    pool_ref[...] = jnp.dot(x, wp_ref[...], preferred_element_type=F32)
    nat_ref[...] = jnp.dot(x, wn_ref[...], preferred_element_type=F32).astype(nat_ref.dtype)
    qT_ref[...] = lax.dot_general(wq_ref[...], x, nt, preferred_element_type=F32).astype(qT_ref.dtype)
    vT = lax.dot_general(wv_ref[...], x, nt, preferred_element_type=F32).astype(vT_ref.dtype)
    slab = vT_ref.shape[2]
    for j in range(vT_ref.shape[0]):
        vT_ref[j] = vT[:, j * slab:(j + 1) * slab]
    wiT_ref[...] = lax.dot_general(ww_ref[...], x, nt, preferred_element_type=F32)


def _projections(x2, w_pool, w_nat, w_qT, w_vT, w_wiT, tm, slab):
    n, d = x2.shape
    full = lambda w: _resident(w.shape, lambda i: (0, 0))
    return pl.pallas_call(
        _proj_kernel,
        grid=(n // tm,),
        in_specs=[pl.BlockSpec((tm, d), lambda i: (i, 0)),
                  full(w_pool), full(w_nat), full(w_qT), full(w_vT), full(w_wiT)],
        out_specs=[pl.BlockSpec((tm, w_pool.shape[1]), lambda i: (i, 0)),
                   pl.BlockSpec((tm, w_nat.shape[1]), lambda i: (i, 0)),
                   pl.BlockSpec((w_qT.shape[0], tm), lambda i: (0, i)),
                   pl.BlockSpec((tm // slab, w_vT.shape[0], slab), lambda i: (i, 0, 0)),
                   pl.BlockSpec((w_wiT.shape[0], tm), lambda i: (0, i))],
        out_shape=[jax.ShapeDtypeStruct((n, w_pool.shape[1]), F32),
                   jax.ShapeDtypeStruct((n, w_nat.shape[1]), BF16),
                   jax.ShapeDtypeStruct((w_qT.shape[0], n), BF16),
                   jax.ShapeDtypeStruct((n // slab, w_vT.shape[0], slab), BF16),
                   jax.ShapeDtypeStruct((w_wiT.shape[0], n), F32)],
        compiler_params=_params(1),
    )(x2, w_pool, w_nat, w_qT, w_vT, w_wiT)


def _pool_kernel(v_ref, halo_ref, w_ref, sc_ref, o_ref, buf_ref, *, seq_tiles, halo):
    ts = v_ref.shape[0]
    group = w_ref.shape[1]
    tile = pl.program_id(0) % seq_tiles
    buf_ref[0:halo, :] = jnp.where(tile == 0, 0.0, halo_ref[...])
    buf_ref[halo:, :] = v_ref[...]
    pos = tile * ts + lax.broadcasted_iota(I32, (ts, 1), 0)
    for g, win in enumerate(POOL_WINDOWS):
        cols = slice(g * group, (g + 1) * group)
        tok = buf_ref[halo:halo + ts, cols]
        acc = tok
        for j in range(1, win):
            acc = acc + buf_ref[halo - j:halo - j + ts, cols]
        cnt = jnp.minimum(pos + 1, win).astype(F32)
        pooled = acc / cnt - tok
        mixed = jnp.dot(pooled.astype(BF16), w_ref[g], preferred_element_type=F32)
        o_ref[:, cols] = (mixed * sc_ref[:, cols]).astype(o_ref.dtype)


def _pool_mixer(p_pool, pool_w, pool_scale, seq, ts):
    n, d_pool = p_pool.shape
    halo = max(POOL_WINDOWS)
    seq_tiles = seq // ts
    return pl.pallas_call(
        functools.partial(_pool_kernel, seq_tiles=seq_tiles, halo=halo),
        grid=(n // ts,),
        in_specs=[pl.BlockSpec((ts, d_pool), lambda r: (r, 0)),
                  pl.BlockSpec((halo, d_pool), lambda r: (jnp.maximum(r * (ts // halo) - 1, 0), 0)),
                  pl.BlockSpec(pool_w.shape, lambda r: (0, 0, 0)),
                  pl.BlockSpec((1, d_pool), lambda r: (0, 0))],
        out_specs=pl.BlockSpec((ts, d_pool), lambda r: (r, 0)),
        out_shape=jax.ShapeDtypeStruct((n, d_pool), BF16),
        scratch_shapes=[pltpu.VMEM((halo + ts, d_pool), F32)],
        compiler_params=_params(1),
    )(p_pool, p_pool, pool_w, pool_scale)


def _bias_kernel(rb_ref, o_ref, *, tq):
    n_heads, n_slabs, ks, _ = o_ref.shape
    max_exact = N_BUCKETS // 2
    for j in range(n_slabs):
        dist = (lax.broadcasted_iota(I32, (ks, tq), 1) + (tq - ks * j)
                - lax.broadcasted_iota(I32, (ks, tq), 0))
        dist = jnp.maximum(dist, 0)
        d = jnp.maximum(dist, 1).astype(F32)
        large = max_exact + (jnp.log(d / max_exact) / math.log(MAX_DISTANCE / max_exact)
                             * (N_BUCKETS - max_exact))
        coord = jnp.where(dist < max_exact, dist.astype(F32), large)
        for h in range(n_heads):
            def pick(b, acc, h=h, coord=coord):
                return jnp.where(coord >= jnp.asarray(b, F32), rb_ref[b, h], acc)
            bias = lax.fori_loop(1, N_BUCKETS, pick, jnp.full((ks, tq), rb_ref[0, h], F32))
            o_ref[h, j] = (bias - rb_ref[N_BUCKETS - 1, h]) * LOG2E


def _bias_tiles(rel_bias, tq):
    n_heads = rel_bias.shape[1]
    return pl.pallas_call(
        functools.partial(_bias_kernel, tq=tq),
        in_specs=[pl.BlockSpec(memory_space=pltpu.SMEM)],
        out_specs=pl.BlockSpec(memory_space=pltpu.VMEM),
        out_shape=jax.ShapeDtypeStruct((n_heads, 2, tq, tq), F32),
        compiler_params=pltpu.CompilerParams(vmem_limit_bytes=VMEM_LIMIT),
    )(rel_bias)


def _attn_kernel(qT_ref, qiT_ref, wiT_ref, k_ref, ki2_ref, vT_ref, bias_ref, o_ref,
                 key_ref, m_ref, l_ref, acc_ref, s_ref, *, top_k, tq, hd):
    ks = LANES
    halves = tq // ks
    qb = pl.program_id(1)
    n_blk = qb + 1
    q_pos = qb * tq + lax.broadcasted_iota(I32, (1, tq), 1)

    def key_pos(r0):
        return r0 + lax.broadcasted_iota(I32, (ks, 1), 0)

    w_all = wiT_ref[...]

    def score_blk(kb, carry):
        lo, hi = carry
        for hf in range(halves):
            r0 = pl.multiple_of(kb * tq + hf * ks, ks)
            acc = jnp.zeros((ks, tq), F32)
            for p in range(IDX_HEADS // 2):
                qp = qiT_ref[p * LANES:(p + 1) * LANES, :]
                z0 = jnp.dot(ki2_ref[pl.ds(r0, ks), 0:LANES], qp, preferred_element_type=F32)
                z1 = jnp.dot(ki2_ref[pl.ds(r0, ks), LANES:2 * LANES], qp,
                             preferred_element_type=F32)
                acc = acc + w_all[2 * p:2 * p + 1, :] * jnp.maximum(z0, 0.0)
                acc = acc + w_all[2 * p + 1:2 * p + 2, :] * jnp.maximum(z1, 0.0)
            causal = key_pos(r0) <= q_pos
            key_ref[pl.ds(r0, ks), :] = jnp.where(causal, acc, NEG_INF)
            hi = jnp.maximum(hi, jnp.max(jnp.where(causal, acc, -jnp.inf), axis=0, keepdims=True))
            lo = jnp.minimum(lo, jnp.min(jnp.where(causal, acc, jnp.inf), axis=0, keepdims=True))
        return lo, hi

    lo, hi = lax.fori_loop(0, n_blk, score_blk,
                           (jnp.full((1, tq), jnp.inf, F32), jnp.full((1, tq), -jnp.inf, F32)))

    def pending(state):
        i, _, _, cnt_lo = state
        return jnp.logical_and(i < BISECT_CAP, jnp.max(cnt_lo) > top_k)

    def bisect(state):
        i, lo, hi, cnt_lo = state
        mid = lo + 0.5 * (hi - lo)

        def count_blk(kb, c8):
            for hf in range(halves):
                r0 = pl.multiple_of(kb * tq + hf * ks, ks)
                hit = (key_ref[pl.ds(r0, ks), :] >= mid).astype(F32)
                c8 = c8 + jnp.sum(hit.reshape(ks // SUBLANES, SUBLANES, tq), axis=0)
            return c8

        c8 = lax.fori_loop(0, n_blk, count_blk, jnp.zeros((SUBLANES, tq), F32))
        cnt = jnp.sum(c8, axis=0, keepdims=True)
        take = cnt >= top_k
        return (i + 1, jnp.where(take, mid, lo), jnp.where(take, hi, mid),
                jnp.where(take, cnt, cnt_lo))

    state = (jnp.int32(0), lo, hi, (q_pos + 1).astype(F32))
    state = lax.fori_loop(0, BISECT_UNCHECKED, lambda _, s: bisect(s), state)
    _, thr, _, _ = lax.while_loop(pending, bisect, state)

    def mask_blk(kb, carry):
        for hf in range(halves):
            r0 = pl.multiple_of(kb * tq + hf * ks, ks)
            sel = (key_ref[pl.ds(r0, ks), :] >= thr) & (key_pos(r0) <= q_pos)
            key_ref[pl.ds(r0, ks), :] = jnp.where(sel, 0.0, NEG_INF).astype(F32)
        return carry

    lax.fori_loop(0, n_blk, mask_blk, 0)

    m_ref[...] = jnp.full(m_ref.shape, M_FLOOR, F32)
    l_ref[...] = jnp.zeros(l_ref.shape, F32)
    acc_ref[...] = jnp.zeros(acc_ref.shape, F32)

    def step(kb, carry, with_bias):
        r0 = pl.multiple_of(kb * tq, tq)
        mask = key_ref[pl.ds(r0, tq), :]
        maxes = []
        for h in range(N_HEADS):
            hs = slice(h * hd, (h + 1) * hd)
            s = jnp.dot(k_ref[pl.ds(r0, tq), hs], qT_ref[hs, :], preferred_element_type=F32) + mask
            if with_bias:
                s = s + bias_ref[h, kb - (qb - 1)]
            s_ref[h] = s
            maxes.append(jnp.max(s, axis=0, keepdims=True))
        for h in range(N_HEADS):
            hs = slice(h * hd, (h + 1) * hd)
            m = m_ref[h]
            m_new = jnp.maximum(m, maxes[h])
            alpha = jnp.exp2(m - m_new)
            p = jnp.exp2(s_ref[h] - m_new)
            l_ref[h] = alpha * l_ref[h] + jnp.sum(p, axis=0, keepdims=True)
            acc_ref[h] = alpha * acc_ref[h] + jnp.dot(vT_ref[kb, hs, :], p.astype(BF16),
                                                      preferred_element_type=F32)
            m_ref[h] = m_new
        return carry

    near0 = jnp.maximum(qb - 1, 0)
    lax.fori_loop(0, near0, functools.partial(step, with_bias=False), 0)
    lax.fori_loop(near0, n_blk, functools.partial(step, with_bias=True), 0)
    for h in range(N_HEADS):
        o_ref[h * hd:(h + 1) * hd, :] = (acc_ref[h] / l_ref[h]).astype(o_ref.dtype)


def _layer_norm_t(r, g, b):
    mu = jnp.mean(r, axis=0, keepdims=True)
    c = r - mu
    var = jnp.mean(c * c, axis=0, keepdims=True)
    return c * lax.rsqrt(var + LN_EPS) * g + b


def _outproj_kernel(pool_ref, attnT_ref, x_ref, wpT_ref, waT_ref, g_ref, b_ref, hT_ref, hTb_ref,
                    *, alpha):
    mix = lax.dot_general(wpT_ref[...], pool_ref[...], (((1,), (1,)), ((), ())),
                          preferred_element_type=F32)
    mix = mix + jnp.dot(waT_ref[...], attnT_ref[...], preferred_element_type=F32)
    h = _layer_norm_t(alpha * x_ref[...].T + mix, g_ref[...], b_ref[...])
    hT_ref[...] = h
    hTb_ref[...] = h.astype(BF16)


def _out_projection(pool_out, attnT, x2, wpT, waT, g, b, alpha, tm):
    n, d = x2.shape
    d_pool = pool_out.shape[1]
    d_attn = attnT.shape[0]
    return pl.pallas_call(
        functools.partial(_outproj_kernel, alpha=alpha),
        grid=(n // tm,),
        in_specs=[pl.BlockSpec((tm, d_pool), lambda i: (i, 0)),
                  pl.BlockSpec((d_attn, tm), lambda i: (0, i)),
                  pl.BlockSpec((tm, d), lambda i: (i, 0)),
                  _resident((d, d_pool), lambda i: (0, 0)),
                  _resident((d, d_attn), lambda i: (0, 0)),
                  _resident((d, 1), lambda i: (0, 0)),
                  _resident((d, 1), lambda i: (0, 0))],
        out_specs=[pl.BlockSpec((d, tm), lambda i: (0, i)),
                   pl.BlockSpec((d, tm), lambda i: (0, i))],
        out_shape=[jax.ShapeDtypeStruct((d, n), F32), jax.ShapeDtypeStruct((d, n), BF16)],
        compiler_params=_params(1),
    )(pool_out, attnT, x2, wpT, waT, g, b)


def _peer_query_kernel(hTb_ref, wqT_ref, sk_ref, sT_ref):
    qhT = jnp.dot(wqT_ref[...], hTb_ref[...], preferred_element_type=F32).astype(BF16)
    half = sk_ref.shape[2]
    for c in range(sT_ref.shape[0]):
        sT_ref[c] = jnp.dot(sk_ref[c % 2], qhT[c * half:(c + 1) * half, :],
                            preferred_element_type=F32)


def _peer_query(hTb, wqT, sub_keys, tm):
    d, n = hTb.shape
    dq = wqT.shape[0]
    n_keys, half = sub_keys.shape[1:]
    n_slices = dq // half
    return pl.pallas_call(
        _peer_query_kernel,
        grid=(n // tm,),
        in_specs=[pl.BlockSpec((d, tm), lambda i: (0, i)),
                  _resident((dq, d), lambda i: (0, 0)),
                  _resident(sub_keys.shape, lambda i: (0, 0, 0))],
        out_specs=pl.BlockSpec((n_slices, n_keys, tm), lambda i: (0, 0, i)),
        out_shape=jax.ShapeDtypeStruct((n_slices, n_keys, n), F32),
        compiler_params=_params(1),
    )(hTb, wqT, sub_keys)


def _bitonic_sort_desc(xs):
    xs = list(xs)
    n = len(xs)
    k = 2
    while k <= n:
        j = k // 2
        while j >= 1:
            for i in range(n):
                o = i ^ j
                if o > i:
                    hi, lo = jnp.maximum(xs[i], xs[o]), jnp.minimum(xs[i], xs[o])
                    xs[i], xs[o] = (hi, lo) if (i & k) == 0 else (lo, hi)
            j //= 2
        k *= 2
    return xs


def _bitonic_merge_desc(xs):
    xs = list(xs)
    n = len(xs)
    j = n // 2
    while j >= 1:
        for i in range(n):
            o = i ^ j
            if o > i:
                xs[i], xs[o] = jnp.maximum(xs[i], xs[o]), jnp.minimum(xs[i], xs[o])
        j //= 2
    return xs


def _top_of_two_sorted(a, b):
    n = len(a)
    return [jnp.maximum(a[i], b[n - 1 - i]) for i in range(n)]


def _top16_over_keys(s):
    groups = s.shape[0] // SUBLANES
    xs = _bitonic_sort_desc([s[g * SUBLANES:(g + 1) * SUBLANES, :] for g in range(groups)])
    shift = SUBLANES // 2
    while shift >= 1:
        other = [pltpu.roll(a, shift, 0) for a in xs]
        xs = _bitonic_merge_desc(_top_of_two_sorted(xs, other))
        shift //= 2
    return xs


def _peer_route_kernel(sT_ref, c1_ref, e1_ref, r2_ref, e2_ref):
    k = PEER_TOPK
    assert PEER_HEADS == SUBLANES, "one head per sublane in the candidate stage"
    sub = lax.broadcasted_iota(I32, (SUBLANES, sT_ref.shape[2]), 0)
    a, b = None, None
    for h in range(PEER_HEADS):
        ah = _top16_over_keys(sT_ref[2 * h])
        bh = _top16_over_keys(sT_ref[2 * h + 1])
        a = ah if a is None else [jnp.where(sub == h, x, y) for x, y in zip(ah, a)]
        b = bh if b is None else [jnp.where(sub == h, x, y) for x, y in zip(bh, b)]
    pairs = [(i, j) for i in range(k) for j in range(k) if (i + 1) * (j + 1) <= k]
    cands = [a[i] + b[j] for i, j in pairs]
    pad = jnp.full_like(cands[0], -jnp.inf)
    padded = cands + [pad] * (-len(cands) % k)
    top = _bitonic_sort_desc(padded[:k])
    for g in range(1, len(padded) // k):
        nxt = _bitonic_sort_desc(padded[g * k:(g + 1) * k])
        top = _top_of_two_sorted(top, nxt)
        if g + 1 < len(padded) // k:
            top = _bitonic_merge_desc(top)
    cmax = a[0] + b[0]
    thr = functools.reduce(jnp.minimum, top)
    z = functools.reduce(jnp.add, [jnp.exp(t - cmax) for t in top])
    log2z = jnp.log(z) * LOG2E
    cut = []
    for r in range(k):
        hits = [jnp.where(c >= thr, 1.0, 0.0) for c, (i, j) in zip(cands, pairs) if i == r]
        cut.append(functools.reduce(jnp.add, hits) - 1.0)
    for h in range(PEER_HEADS):
        row = slice(h, h + 1)
        s1 = sT_ref[2 * h]
        s2 = sT_ref[2 * h + 1]
        c1 = jnp.full(s1.shape, -1.0, F32)
        r2 = jnp.full(s2.shape, float(k), F32)
        for r in reversed(range(k)):
            c1 = jnp.where(s1 == a[r][row, :], cut[r][row, :], c1)
            r2 = jnp.where(s2 == b[r][row, :], float(r), r2)
        c1_ref[h] = c1
        r2_ref[h] = r2.astype(BF16)
        e1_ref[h] = jnp.exp2((s1 - a[0][row, :]) * LOG2E)
        e2_ref[h] = jnp.exp2((s2 - b[0][row, :]) * LOG2E - log2z[row, :]).astype(BF16)


def _peer_route(sT, tt):
    n_slices, n_keys, n = sT.shape
    big = pl.BlockSpec((PEER_HEADS, n_keys, tt), lambda i: (0, 0, i))
    shape = (PEER_HEADS, n_keys, n)
    return pl.pallas_call(
        _peer_route_kernel,
        grid=(n // tt,),
        in_specs=[pl.BlockSpec((n_slices, n_keys, tt), lambda i: (0, 0, i))],
        out_specs=[big, big, big, big],
        out_shape=[jax.ShapeDtypeStruct(shape, F32), jax.ShapeDtypeStruct(shape, F32),
                   jax.ShapeDtypeStruct(shape, BF16), jax.ShapeDtypeStruct(shape, BF16)],
        compiler_params=_params(1),
    )(sT)


def _peer_dense_kernel(hTb_ref, u_ref, vT_ref, c1_ref, e1_ref, r2_ref, e2_ref, o_ref, *, bounds):
    ei = pl.program_id(1)
    te = u_ref.shape[0]
    rows = te // PEER_KEYS
    spans = list(zip(bounds[:-1], bounds[1:]))

    @pl.when(ei == 0)
    def _():
        o_ref[...] = jnp.zeros_like(o_ref)

    i0 = pl.multiple_of(ei * rows, rows)

    def pre_act(c):
        lo, hi = spans[c]
        return jnp.dot(u_ref[lo:hi, :], hTb_ref[...], preferred_element_type=F32)

    act_next = pre_act(0)
    for c, (lo, hi) in enumerate(spans):
        act = act_next
        if c + 1 < len(spans):
            act_next = pre_act(c + 1)
        gates = []
        for r in range(lo // PEER_KEYS, hi // PEER_KEYS):
            gate = jnp.zeros((PEER_KEYS, act.shape[1]), BF16)
            for h in range(PEER_HEADS):
                cut = c1_ref[h, pl.ds(i0, rows), :][r:r + 1, :].astype(BF16)
                e1 = e1_ref[h, pl.ds(i0, rows), :][r:r + 1, :].astype(BF16)
                gate = gate + jnp.where(r2_ref[h] <= cut, e2_ref[h] * e1, jnp.zeros_like(gate))
            gates.append(gate)
        act = 0.5 * act * (1.0 + lax.erf(act * math.sqrt(0.5)))
        p = jnp.concatenate(
            [g * act[rr * PEER_KEYS:(rr + 1) * PEER_KEYS, :].astype(BF16)
             for rr, g in enumerate(gates)], axis=0)
        o_ref[...] += jnp.dot(vT_ref[:, lo:hi], p, preferred_element_type=F32)


def _peer_dense(hTb, u, vT, c1, e1, r2, e2, tt, te, bounds):
    d, n = hTb.shape
    n_exp = u.shape[0]
    n_keys = c1.shape[1]
    big = pl.BlockSpec((PEER_HEADS, n_keys, tt), lambda t, e: (0, 0, t))
    return pl.pallas_call(
        functools.partial(_peer_dense_kernel, bounds=bounds),
        grid=(n // tt, n_exp // te),
        in_specs=[pl.BlockSpec((d, tt), lambda t, e: (0, t)),
                  pl.BlockSpec((te, d), lambda t, e: (e, 0)),
                  pl.BlockSpec((d, te), lambda t, e: (0, e)),
                  big, big, big, big],
        out_specs=pl.BlockSpec((d, tt), lambda t, e: (0, t)),
        out_shape=jax.ShapeDtypeStruct((d, n), F32),
        compiler_params=_params(2),
    )(hTb, u, vT, c1, e1, r2, e2)


def _final_ln_kernel(hT_ref, fT_ref, g_ref, b_ref, o_ref, *, alpha):
    y = _layer_norm_t(alpha * hT_ref[...] + fT_ref[...], g_ref[...], b_ref[...])
    o_ref[...] = y.T


def _final_ln(hT, ffnT, g, b, alpha, tm):
    d, n = hT.shape
    return pl.pallas_call(
        functools.partial(_final_ln_kernel, alpha=alpha),
        grid=(n // tm,),
        in_specs=[pl.BlockSpec((d, tm), lambda i: (0, i)),
                  pl.BlockSpec((d, tm), lambda i: (0, i)),
                  _resident((d, 1), lambda i: (0, 0)),
                  _resident((d, 1), lambda i: (0, 0))],
        out_specs=pl.BlockSpec((tm, d), lambda i: (i, 0)),
        out_shape=jax.ShapeDtypeStruct((n, d), F32),
        compiler_params=_params(1),
    )(hT, ffnT, g, b)


def _tile(n, pref):
    t = min(pref, n)
    while n % t:
        t -= LANES
    return t


def _layer(h2, batch, seq, w_in, pool_w, pool_scale, bias, w_out, ln1_g, ln1_b,
           peer_wq, sub_keys, peer_u, peer_v, ln2_g, ln2_b, alpha, tq):
    n, d = h2.shape
    d_pool = pool_w.shape[0] * pool_w.shape[1]
    d_attn = w_out.shape[0] - d_pool
    hd = d_attn // N_HEADS
    d_qi = IDX_HEADS * IDX_DIM
    top_k = min(MAX_TOPK, seq // 4)

    o = 0
    w_pool = w_in[:, o:o + d_pool]; o += d_pool
    w_q = w_in[:, o:o + d_attn] * (hd ** -0.5 * LOG2E); o += d_attn
    w_k = w_in[:, o:o + d_attn]; o += d_attn
    w_v = w_in[:, o:o + d_attn]; o += d_attn
    w_qi = w_in[:, o:o + d_qi]; o += d_qi
    w_ki = w_in[:, o:o + IDX_DIM]; o += IDX_DIM
    w_wi = w_in[:, o:o + IDX_HEADS] * (IDX_HEADS * IDX_DIM) ** -0.5
    zeros = jnp.zeros_like(w_ki)
    w_nat = jnp.concatenate([w_k, w_ki, zeros, zeros, w_ki], axis=1).astype(BF16)
    w_qT = jnp.concatenate([w_q.T, w_qi.T], axis=0).astype(BF16)

    p_pool, k_nat, qT_all, vT_blk, wiT = _projections(
        h2, w_pool.astype(BF16), w_nat, w_qT, w_v.T.astype(BF16), w_wi.T.astype(BF16),
        _tile(seq, 512), tq)

    pool_out = _pool_mixer(p_pool, pool_w.astype(BF16), pool_scale.reshape(1, d_pool), seq,
                           _tile(seq, 256))

    attnT = _sparse_attention_split(qT_all, wiT, k_nat, vT_blk, bias, batch, seq, tq, top_k, d_attn)

    w_pT = w_out[:d_pool].T.astype(BF16)
    w_aT = w_out[d_pool:].T.astype(BF16)
    hT, hTb = _out_projection(pool_out, attnT, h2, w_pT, w_aT, ln1_g.reshape(d, 1),
                              ln1_b.reshape(d, 1), alpha, _tile(n, 256))

    sT = _peer_query(hTb, peer_wq.T.astype(BF16), sub_keys.astype(BF16), _tile(n, 512))
    c1, e1, r2, e2 = _peer_route(sT, _tile(n, 256))
    te = 8 * PEER_KEYS
    ffnT = _peer_dense(hTb, peer_u.astype(BF16), peer_v.T.astype(BF16), c1, e1, r2, e2,
                       _tile(n, 512), te, (0, te // 4, 3 * te // 4, te))
    return _final_ln(hT, ffnT, ln2_g.reshape(d, 1), ln2_b.reshape(d, 1), alpha, _tile(n, 256))


def _sparse_attention_split(qT_all, wiT, k_nat, vT_blk, bias, batch, seq, tq, top_k, d_attn):
    n = batch * seq
    nq = seq // tq
    hd = d_attn // N_HEADS
    d_qi = IDX_HEADS * IDX_DIM
    assert d_qi == d_attn, "q and indexer-q row blocks must have equal height"
    tok0 = lambda b, q: (0, b * nq + q)
    tok1 = lambda b, q: (1, b * nq + q)
    return pl.pallas_call(
        functools.partial(_attn_kernel, top_k=top_k, tq=tq, hd=hd),
        grid=(batch, nq),
        in_specs=[pl.BlockSpec((d_attn, tq), tok0),
                  pl.BlockSpec((d_qi, tq), tok1),
                  pl.BlockSpec((IDX_HEADS, tq), tok0),
                  pl.BlockSpec((seq, d_attn), lambda b, q: (b, 0)),
                  pl.BlockSpec((seq, 4 * IDX_DIM), lambda b, q: (b, d_attn // (4 * IDX_DIM))),
                  pl.BlockSpec((nq, d_attn, tq), lambda b, q: (b, 0, 0)),
                  _resident(bias.shape, lambda b, q: (0, 0, 0, 0))],
        out_specs=pl.BlockSpec((d_attn, tq), tok0),
        out_shape=jax.ShapeDtypeStruct((d_attn, n), BF16),
        scratch_shapes=[pltpu.VMEM((seq, tq), F32),
                        pltpu.VMEM((N_HEADS, 1, tq), F32),
                        pltpu.VMEM((N_HEADS, 1, tq), F32),
                        pltpu.VMEM((N_HEADS, hd, tq), F32),
                        pltpu.VMEM((N_HEADS, tq, tq), F32)],
        compiler_params=_params(2),
    )(qT_all, qT_all, wiT, k_nat, k_nat, vT_blk, bias)


def kernel(x, w_in, pool_w, pool_scale, rel_bias, w_out, ln1_g, ln1_b, peer_wq, peer_subkeys,
           peer_u, peer_v, ln2_g, ln2_b):
    batch, seq, d = x.shape
    depth = w_in.shape[0]
    alpha = (2 * depth) ** 0.25
    tq = min(256, seq)
    bias = _bias_tiles(rel_bias, tq)
    h2 = x.reshape(batch * seq, d)
    for l in range(depth):
        h2 = _layer(h2, batch, seq, w_in[l], pool_w[l], pool_scale[l], bias, w_out[l],
                    ln1_g[l], ln1_b[l], peer_wq[l], peer_subkeys[l], peer_u[l], peer_v[l],
                    ln2_g[l], ln2_b[l], alpha, tq)
    return h2.reshape(batch, seq, d)
```
